```python
import jax, jax.numpy as jnp
from jax import lax
import numpy as np

D_MODEL = 2048
BATCH = 4
SEQ = 2048
DEPTH = 2

N_A_LAYERS = max(1, DEPTH // 2)
N_B_LAYERS = DEPTH - N_A_LAYERS
D_SGU = D_MODEL
SGU_CHUNK = 128
SGU_GROUPS = 16
SGU_GROUP_DIM = D_SGU // SGU_GROUPS
HEAD_DIM = 128
N_HEADS = D_MODEL // HEAD_DIM
MOBA_BLOCK = 256
MOBA_TOPK = 3
Q_CHUNK = 8
D_FF = -(-8 * D_MODEL // (3 * 256)) * 256
EPS = 1e-6

kernel_name = "yoco_sgu_moba_hybrid"


def rms_norm(x, g):
    xf = x.astype(jnp.float32)
    y = xf * lax.rsqrt(jnp.mean(xf * xf, axis=-1, keepdims=True) + EPS)
    return (y * g.astype(jnp.float32)).astype(x.dtype)


def layer_norm(x, g, b):
    xf = x.astype(jnp.float32)
    mu = jnp.mean(xf, axis=-1, keepdims=True)
    var = jnp.mean(jnp.square(xf - mu), axis=-1, keepdims=True)
    y = (xf - mu) * lax.rsqrt(var + EPS)
    return (y * g.astype(jnp.float32) + b.astype(jnp.float32)).astype(x.dtype)


def swiglu_ffn(x, norm_g, w_gate, w_up, w_down):
    h = rms_norm(x, norm_g)
    return (jax.nn.silu(h @ w_gate) * (h @ w_up)) @ w_down


def chunked_sgu_mixer(x, norm_g, w_in, ln_g, ln_b, w_s, b_s, w_out):
    B, S, _ = x.shape
    n_chunks = S // SGU_CHUNK
    z = jax.nn.gelu(rms_norm(x, norm_g) @ w_in)
    u, v = jnp.split(z, 2, axis=-1)
    v = layer_norm(v, ln_g, ln_b)
    v = v.reshape(B, n_chunks, SGU_CHUNK, SGU_GROUPS, SGU_GROUP_DIM)
    causal = jnp.tril(jnp.ones((SGU_CHUNK, SGU_CHUNK), dtype=w_s.dtype))
    w = w_s * causal
    mixed = jnp.einsum('gts,bcsgd->bctgd', w, v) + b_s.T[None, None, :, :, None]
    gate = mixed.reshape(B, S, D_SGU)
    return (u * gate) @ w_out


def shared_kv(x, kv_norm, w_k, w_v, k_norm):
    B, S, _ = x.shape
    h = rms_norm(x, kv_norm)
    k = rms_norm((h @ w_k).reshape(B, S, N_HEADS, HEAD_DIM), k_norm).transpose(0, 2, 1, 3)
    v = (h @ w_v).reshape(B, S, N_HEADS, HEAD_DIM).transpose(0, 2, 1, 3)
    n_blocks = -(-S // MOBA_BLOCK)
    pad = n_blocks * MOBA_BLOCK - S
    k = jnp.pad(k, ((0, 0), (0, 0), (0, pad), (0, 0)))
    v = jnp.pad(v, ((0, 0), (0, 0), (0, pad), (0, 0)))
    k_blk = k.reshape(B, N_HEADS, n_blocks, MOBA_BLOCK, HEAD_DIM)
    v_blk = v.reshape(B, N_HEADS, n_blocks, MOBA_BLOCK, HEAD_DIM)
    k_mean = jnp.mean(k_blk.astype(jnp.float32), axis=3).astype(k_blk.dtype)
    return k_blk, v_blk, k_mean


def moba_attention(x, norm_g, w_q, q_norm, w_o, k_blk, v_blk, k_mean):
    B, S, _ = x.shape
    n_blocks = k_blk.shape[2]
    n_sel = min(MOBA_TOPK, n_blocks - 1)
    n_chunks = S // Q_CHUNK
    q = rms_norm((rms_norm(x, norm_g) @ w_q).reshape(B, S, N_HEADS, HEAD_DIM), q_norm)
    q = (q * (HEAD_DIM ** -0.5)).transpose(0, 2, 1, 3)
    q_c = q.reshape(B, N_HEADS, n_chunks, Q_CHUNK, HEAD_DIM).transpose(2, 0, 1, 3, 4)
    xs = (jnp.arange(n_chunks, dtype=jnp.int32), q_c)
    if n_sel > 0:
        q_block = jnp.arange(S) // MOBA_BLOCK
        gate = jnp.einsum('bhsd,bhnd->bhsn', q, k_mean).astype(jnp.float32)
        past = jnp.arange(n_blocks)[None, :] < q_block[:, None]
        gate = jnp.where(past, gate, -jnp.inf)
        top_val, top_idx = lax.top_k(gate, n_sel)
        valid = jnp.isfinite(top_val)
        to_chunks = lambda a: a.reshape(B, N_HEADS, n_chunks, Q_CHUNK, n_sel).transpose(2, 0, 1, 3, 4)
        xs = xs + (to_chunks(top_idx), to_chunks(valid))
    b_idx = jnp.arange(B)[:, None, None, None]
    h_idx = jnp.arange(N_HEADS)[None, :, None, None]

    def attend(args):
        c, qc = args[0], args[1]
        q_pos = c * Q_CHUNK + jnp.arange(Q_CHUNK)
        blk = (c * Q_CHUNK) // MOBA_BLOCK
        k_own = lax.dynamic_index_in_dim(k_blk, blk, axis=2, keepdims=False)
        v_own = lax.dynamic_index_in_dim(v_blk, blk, axis=2, keepdims=False)
        k_pos = blk * MOBA_BLOCK + jnp.arange(MOBA_BLOCK)
        s_own = jnp.einsum('bhqd,bhkd->bhqk', qc, k_own).astype(jnp.float32)
        s_own = jnp.where(k_pos[None, :] <= q_pos[:, None], s_own, -jnp.inf)
        if n_sel == 0:
            p = jax.nn.softmax(s_own, axis=-1).astype(qc.dtype)
            return jnp.einsum('bhqk,bhkd->bhqd', p, v_own)
        idx, ok = args[2], args[3]
        k_sel = k_blk[b_idx, h_idx, idx]
        v_sel = v_blk[b_idx, h_idx, idx]
        s_sel = jnp.einsum('bhqd,bhqnkd->bhqnk', qc, k_sel).astype(jnp.float32)
        s_sel = jnp.where(ok[..., None], s_sel, -jnp.inf)
        s_sel = s_sel.reshape(B, N_HEADS, Q_CHUNK, n_sel * MOBA_BLOCK)
        p = jax.nn.softmax(jnp.concatenate([s_own, s_sel], axis=-1), axis=-1).astype(qc.dtype)
        p_own = p[..., :MOBA_BLOCK]
        p_sel = p[..., MOBA_BLOCK:].reshape(B, N_HEADS, Q_CHUNK, n_sel, MOBA_BLOCK)
        return (jnp.einsum('bhqk,bhkd->bhqd', p_own, v_own)
                + jnp.einsum('bhqnk,bhqnkd->bhqd', p_sel, v_sel))

    out = lax.map(attend, xs)
    out = out.transpose(1, 0, 3, 2, 4).reshape(B, S, D_MODEL)
    return out @ w_o


def setup_inputs(seed: int = 0) -> dict:
    key = jax.random.key(seed)
    ks = iter(jax.random.split(key, 32))
    f32 = jnp.float32
    nrm = lambda shape, scale: jax.random.normal(next(ks), shape, f32) * scale
    gain = lambda shape: 1.0 + 0.02 * jax.random.normal(next(ks), shape, f32)
    return {
        "x": jax.random.normal(next(ks), (BATCH, SEQ, D_MODEL), f32),
        "a_norm": gain((N_A_LAYERS, D_MODEL)),
        "a_w_in": nrm((N_A_LAYERS, D_MODEL, 2 * D_SGU), D_MODEL ** -0.5),
        "a_ln_g": gain((N_A_LAYERS, D_SGU)),
        "a_ln_b": nrm((N_A_LAYERS, D_SGU), 0.02),
        "a_w_s": nrm((N_A_LAYERS, SGU_GROUPS, SGU_CHUNK, SGU_CHUNK), SGU_CHUNK ** -0.5),
        "a_b_s": gain((N_A_LAYERS, SGU_GROUPS, SGU_CHUNK)),
        "a_w_out": nrm((N_A_LAYERS, D_SGU, D_MODEL), D_SGU ** -0.5),
        "kv_norm": gain((D_MODEL,)),
        "w_k": nrm((D_MODEL, D_MODEL), D_MODEL ** -0.5),
        "w_v": nrm((D_MODEL, D_MODEL), D_MODEL ** -0.5),
        "k_norm": gain((HEAD_DIM,)),
        "b_norm": gain((N_B_LAYERS, D_MODEL)),
        "b_w_q": nrm((N_B_LAYERS, D_MODEL, D_MODEL), D_MODEL ** -0.5),
        "b_q_norm": gain((N_B_LAYERS, HEAD_DIM)),
        "b_w_o": nrm((N_B_LAYERS, D_MODEL, D_MODEL), D_MODEL ** -0.5),
        "ffn_norm": gain((DEPTH, D_MODEL)),
        "ffn_w_gate": nrm((DEPTH, D_MODEL, D_FF), D_MODEL ** -0.5),
        "ffn_w_up": nrm((DEPTH, D_MODEL, D_FF), D_MODEL ** -0.5),
        "ffn_w_down": nrm((DEPTH, D_FF, D_MODEL), D_FF ** -0.5),
    }


def reference(x, a_norm, a_w_in, a_ln_g, a_ln_b, a_w_s, a_b_s, a_w_out,
              kv_norm, w_k, w_v, k_norm, b_norm, b_w_q, b_q_norm, b_w_o,
              ffn_norm, ffn_w_gate, ffn_w_up, ffn_w_down):
    kv = None
    for layer in range(DEPTH):
        if layer < N_A_LAYERS:
            i = layer
            x = x + chunked_sgu_mixer(x, a_norm[i], a_w_in[i], a_ln_g[i], a_ln_b[i],
                                      a_w_s[i], a_b_s[i], a_w_out[i])
        else:
            if kv is None:
                kv = shared_kv(x, kv_norm, w_k, w_v, k_norm)
            i = layer - N_A_LAYERS
            x = x + moba_attention(x, b_norm[i], b_w_q[i], b_q_norm[i], b_w_o[i], *kv)
        x = x + swiglu_ffn(x, ffn_norm[layer], ffn_w_gate[layer], ffn_w_up[layer], ffn_w_down[layer])
    return x
```

```python
import functools

import jax
import jax.numpy as jnp
from jax import lax
from jax.experimental import pallas as pl
from jax.experimental.pallas import tpu as pltpu

F32 = jnp.float32
BF16 = jnp.bfloat16
EPS = 1e-6

SGU_CHUNK = 128
SGU_GROUPS = 16
HEAD_DIM = 128
MOBA_BLOCK = 256
MOBA_TOPK = 3

VMEM_LIMIT_BYTES = 56 * 1024 * 1024
ROW_TILE = 1024
ROW_BLOCK = 256
NORM_ROWS = 128


def _params(*sem):
    return pltpu.CompilerParams(dimension_semantics=sem, vmem_limit_bytes=VMEM_LIMIT_BYTES)


def _rows(n_rows, block, body):
    def step(r, carry):
        body(pl.multiple_of(r * block, block))
        return carry
    lax.fori_loop(0, n_rows // block, step, 0)


def _rms_scale(x):
    return x * lax.rsqrt(jnp.mean(x * x, axis=-1, keepdims=True) + EPS)


def _sgu_in_kernel(x_ref, g_ref, w_ref, lng_ref, lnb_ref, u_ref, v_ref, h_scr, v_scr,
                   *, tm, tn, n_u, n_v):
    j = pl.program_id(1)

    @pl.when(j == 0)
    def _():
        def norm(r0):
            x = x_ref[pl.ds(r0, NORM_ROWS), :]
            h_scr[pl.ds(r0, NORM_ROWS), :] = (_rms_scale(x) * g_ref[...]).astype(BF16)
        _rows(tm, NORM_ROWS, norm)

    def gelu_mm(r0):
        z = jnp.dot(h_scr[pl.ds(r0, ROW_BLOCK), :], w_ref[...].astype(BF16),
                    preferred_element_type=F32)
        return jax.nn.gelu(z, approximate=True)

    @pl.when(j < n_u)
    def _():
        def body(r0):
            u_ref[pl.ds(r0, ROW_BLOCK), :] = gelu_mm(r0).astype(u_ref.dtype)
        _rows(tm, ROW_BLOCK, body)

    @pl.when(j >= n_u)
    def _():
        def body(r0):
            v_scr[j - n_u, pl.ds(r0, ROW_BLOCK), :] = gelu_mm(r0)
        _rows(tm, ROW_BLOCK, body)

    @pl.when(j == n_u + n_v - 1)
    def _():
        d = n_v * tn

        def ln(r0):
            parts = [v_scr[t, pl.ds(r0, NORM_ROWS), :] for t in range(n_v)]
            mu = sum(jnp.sum(p, axis=-1, keepdims=True) for p in parts) / d
            cen = [p - mu for p in parts]
            var = sum(jnp.sum(c * c, axis=-1, keepdims=True) for c in cen) / d
            inv = lax.rsqrt(var + EPS)
            for t in range(n_v):
                cols = slice(t * tn, (t + 1) * tn)
                y = cen[t] * inv * lng_ref[:, cols] + lnb_ref[:, cols]
                v_ref[pl.ds(r0, NORM_ROWS), cols] = y.astype(v_ref.dtype)
        _rows(tm, NORM_ROWS, ln)


def _sgu_in(x2d, norm_g, w_in, ln_g, ln_b):
    t, d = x2d.shape
    d_sgu = w_in.shape[1] // 2
    tm, tn = ROW_TILE, 1024
    n_u = n_v = d_sgu // tn
    kern = functools.partial(_sgu_in_kernel, tm=tm, tn=tn, n_u=n_u, n_v=n_v)
    return pl.pallas_call(
        kern,
        grid=(t // tm, n_u + n_v),
        in_specs=[
            pl.BlockSpec((tm, d), lambda i, j: (i, 0)),
            pl.BlockSpec((1, d), lambda i, j: (0, 0)),
            pl.BlockSpec((d, tn), lambda i, j: (0, j)),
            pl.BlockSpec((1, d_sgu), lambda i, j: (0, 0)),
            pl.BlockSpec((1, d_sgu), lambda i, j: (0, 0)),
        ],
        out_specs=[
            pl.BlockSpec((tm, tn), lambda i, j: (i, jnp.minimum(j, n_u - 1))),
            pl.BlockSpec((tm, d_sgu), lambda i, j: (i, 0)),
        ],
        out_shape=[jax.ShapeDtypeStruct((t, d_sgu), BF16),
                   jax.ShapeDtypeStruct((t, d_sgu), BF16)],
        scratch_shapes=[pltpu.VMEM((tm, d), BF16), pltpu.VMEM((n_v, tm, tn), F32)],
        compiler_params=_params("parallel", "arbitrary"),
        name="sgu_in",
    )(x2d, norm_g.reshape(1, d), w_in, ln_g.reshape(1, d_sgu), ln_b.reshape(1, d_sgu))


def _sgu_gate_kernel(u_ref, v_ref, ws_ref, bs_ref, o_ref, wm_scr, *, tm):
    c = SGU_CHUNK

    @pl.when(pl.program_id(0) == 0)
    def _():
        t_idx = lax.broadcasted_iota(jnp.int32, (c, c), 0)
        s_idx = lax.broadcasted_iota(jnp.int32, (c, c), 1)
        for g in range(SGU_GROUPS):
            wm_scr[g] = jnp.where(s_idx <= t_idx, ws_ref[g], 0.0).astype(BF16)

    def chunk(r0):
        for g in range(SGU_GROUPS):
            cols = slice(g * c, (g + 1) * c)
            mixed = jnp.dot(wm_scr[g], v_ref[pl.ds(r0, c), cols], preferred_element_type=F32)
            gate = mixed + bs_ref[g]
            o_ref[pl.ds(r0, c), cols] = (u_ref[pl.ds(r0, c), cols].astype(F32) * gate).astype(o_ref.dtype)
    _rows(tm, c, chunk)


def _sgu_gate(u, v_ln, w_s, b_s):
    t, d = u.shape
    tm = 512
    kern = functools.partial(_sgu_gate_kernel, tm=tm)
    return pl.pallas_call(
        kern,
        grid=(t // tm,),
        in_specs=[
            pl.BlockSpec((tm, d), lambda i: (i, 0)),
            pl.BlockSpec((tm, d), lambda i: (i, 0)),
            pl.BlockSpec(w_s.shape, lambda i: (0, 0, 0)),
            pl.BlockSpec((SGU_GROUPS, SGU_CHUNK, 1), lambda i: (0, 0, 0)),
        ],
        out_specs=pl.BlockSpec((tm, d), lambda i: (i, 0)),
        out_shape=jax.ShapeDtypeStruct((t, d), BF16),
        scratch_shapes=[pltpu.VMEM((SGU_GROUPS, SGU_CHUNK, SGU_CHUNK), BF16)],
        compiler_params=_params("arbitrary"),
        name="sgu_gate",
    )(u, v_ln, w_s, b_s.reshape(SGU_GROUPS, SGU_CHUNK, 1))


def _mm_res_kernel(a_ref, w_ref, x_ref, o_ref, *, tm):
    def body(r0):
        acc = jnp.dot(a_ref[pl.ds(r0, ROW_BLOCK), :], w_ref[...].astype(BF16),
                      preferred_element_type=F32)
        o_ref[pl.ds(r0, ROW_BLOCK), :] = x_ref[pl.ds(r0, ROW_BLOCK), :] + acc
    _rows(tm, ROW_BLOCK, body)


def _mm_res(a, w, x2d):
    t, k = a.shape
    n = w.shape[1]
    tm, tn = ROW_TILE, 1024
    return pl.pallas_call(
        functools.partial(_mm_res_kernel, tm=tm),
        grid=(t // tm, n // tn),
        in_specs=[
            pl.BlockSpec((tm, k), lambda i, j: (i, 0)),
            pl.BlockSpec((k, tn), lambda i, j: (0, j)),
            pl.BlockSpec((tm, tn), lambda i, j: (i, j)),
        ],
        out_specs=pl.BlockSpec((tm, tn), lambda i, j: (i, j)),
        out_shape=jax.ShapeDtypeStruct((t, n), F32),
        compiler_params=_params("parallel", "arbitrary"),
        name="mm_res",
    )(a, w, x2d)


def _ffn_kernel(x_ref, g_ref, wg_ref, wu_ref, wd_ref, o_ref, h_scr, *, tm):
    j = pl.program_id(1)

    @pl.when(j == 0)
    def _():
        def norm(r0):
            x = x_ref[pl.ds(r0, NORM_ROWS), :]
            h_scr[pl.ds(r0, NORM_ROWS), :] = (_rms_scale(x) * g_ref[...]).astype(BF16)
            o_ref[pl.ds(r0, NORM_ROWS), :] = x
        _rows(tm, NORM_ROWS, norm)

    def body(r0):
        hb = h_scr[pl.ds(r0, ROW_BLOCK), :]
        gate = jnp.dot(hb, wg_ref[...].astype(BF16), preferred_element_type=F32)
        up = jnp.dot(hb, wu_ref[...].astype(BF16), preferred_element_type=F32)
        act = (jax.nn.silu(gate) * up).astype(BF16)
        o_ref[pl.ds(r0, ROW_BLOCK), :] += jnp.dot(act, wd_ref[...].astype(BF16),
                                                  preferred_element_type=F32)
    _rows(tm, ROW_BLOCK, body)


def _ffn(x2d, norm_g, w_gate, w_up, w_down):
    t, d = x2d.shape
    f = w_gate.shape[1]
    tm, tf = ROW_TILE, 512
    return pl.pallas_call(
        functools.partial(_ffn_kernel, tm=tm),
        grid=(t // tm, f // tf),
        in_specs=[
            pl.BlockSpec((tm, d), lambda i, j: (i, 0)),
            pl.BlockSpec((1, d), lambda i, j: (0, 0)),
            pl.BlockSpec((d, tf), lambda i, j: (0, j)),
            pl.BlockSpec((d, tf), lambda i, j: (0, j)),
            pl.BlockSpec((tf, d), lambda i, j: (j, 0)),
        ],
        out_specs=pl.BlockSpec((tm, d), lambda i, j: (i, 0)),
        out_shape=jax.ShapeDtypeStruct((t, d), F32),
        scratch_shapes=[pltpu.VMEM((tm, d), BF16)],
        compiler_params=_params("parallel", "arbitrary"),
        name="ffn",
    )(x2d, norm_g.reshape(1, d), w_gate, w_up, w_down)


def _kvq_kernel(x_ref, gkv_ref, gq_ref, w_ref, nk_ref, nq_ref, kq_ref, vt_ref, km_ref,
                hkv_scr, hq_scr, *, tm, tn, n_proj, q_scale):
    j = pl.program_id(1)
    heads = tn // HEAD_DIM
    blocks = tm // MOBA_BLOCK

    @pl.when(j == 0)
    def _():
        def norm(r0):
            y = _rms_scale(x_ref[pl.ds(r0, NORM_ROWS), :])
            hkv_scr[pl.ds(r0, NORM_ROWS), :] = (y * gkv_ref[...]).astype(BF16)
            hq_scr[pl.ds(r0, NORM_ROWS), :] = (y * gq_ref[...]).astype(BF16)
        _rows(tm, NORM_ROWS, norm)

    def proj(h_scr, blk):
        rows = slice(blk * MOBA_BLOCK, (blk + 1) * MOBA_BLOCK)
        return rows, jnp.dot(h_scr[rows, :], w_ref[...].astype(BF16), preferred_element_type=F32)

    @pl.when(j < n_proj)
    def _():
        for blk in range(blocks):
            rows, y = proj(hkv_scr, blk)
            for hh in range(heads):
                cols = slice(hh * HEAD_DIM, (hh + 1) * HEAD_DIM)
                kh = _rms_scale(y[:, cols]) * nk_ref[...]
                kq_ref[rows, cols] = kh.astype(kq_ref.dtype)
                km_ref[0, blk:blk + 1, cols] = jnp.mean(kh, axis=0, keepdims=True)

    @pl.when(jnp.logical_and(j >= n_proj, j < 2 * n_proj))
    def _():
        for blk in range(blocks):
            rows, y = proj(hkv_scr, blk)
            vt_ref[:, rows] = y.T.astype(vt_ref.dtype)

    @pl.when(j >= 2 * n_proj)
    def _():
        for blk in range(blocks):
            rows, y = proj(hq_scr, blk)
            for hh in range(heads):
                cols = slice(hh * HEAD_DIM, (hh + 1) * HEAD_DIM)
                qh = _rms_scale(y[:, cols]) * nq_ref[...] * q_scale
                kq_ref[rows, cols] = qh.astype(kq_ref.dtype)


def _kvq(x2d, kv_norm, q_in_norm, w_kvq, k_norm, q_norm):
    t, d = x2d.shape
    tm, tn = ROW_TILE, 1024
    n_proj = d // tn
    blocks = tm // MOBA_BLOCK
    kern = functools.partial(_kvq_kernel, tm=tm, tn=tn, n_proj=n_proj, q_scale=HEAD_DIM ** -0.5)

    def kq_map(i, j):
        return i, jnp.where(j < n_proj, j, jnp.maximum(j - n_proj, n_proj - 1))

    kq, vt, km = pl.pallas_call(
        kern,
        grid=(t // tm, 3 * n_proj),
        in_specs=[
            pl.BlockSpec((tm, d), lambda i, j: (i, 0)),
            pl.BlockSpec((1, d), lambda i, j: (0, 0)),
            pl.BlockSpec((1, d), lambda i, j: (0, 0)),
            pl.BlockSpec((d, tn), lambda i, j: (0, j)),
            pl.BlockSpec((1, HEAD_DIM), lambda i, j: (0, 0)),
            pl.BlockSpec((1, HEAD_DIM), lambda i, j: (0, 0)),
        ],
        out_specs=[
            pl.BlockSpec((tm, tn), kq_map),
            pl.BlockSpec((tn, tm), lambda i, j: (jnp.clip(j - n_proj, 0, n_proj - 1), i)),
            pl.BlockSpec((1, blocks, tn), lambda i, j: (i, 0, jnp.minimum(j, n_proj - 1))),
        ],
        out_shape=[jax.ShapeDtypeStruct((t, 2 * d), BF16),
                   jax.ShapeDtypeStruct((d, t), BF16),
                   jax.ShapeDtypeStruct((t // tm, blocks, d), F32)],
        scratch_shapes=[pltpu.VMEM((tm, d), BF16), pltpu.VMEM((tm, d), BF16)],
        compiler_params=_params("parallel", "arbitrary"),
        name="kvq",
    )(x2d, kv_norm.reshape(1, d), q_in_norm.reshape(1, d), w_kvq,
      k_norm.reshape(1, HEAD_DIM), q_norm.reshape(1, HEAD_DIM))
    return kq, vt, km.reshape(t // MOBA_BLOCK, d)


_NT = (((1,), (1,)), ((), ()))


def _attn_kernel(q_ref, k_ref, vt_ref, km_ref, o_ref, *, n_blocks):
    bs = MOBA_BLOCK
    km = km_ref[...].astype(BF16)
    blk_id = lax.broadcasted_iota(jnp.int32, (n_blocks, bs), 0)
    k_pos = lax.broadcasted_iota(jnp.int32, (bs, bs), 0)
    q_pos = lax.broadcasted_iota(jnp.int32, (bs, bs), 1)
    neg_inf = -jnp.inf

    for i in range(n_blocks):
        q = q_ref[i * bs:(i + 1) * bs, :]
        nk = (i + 1) * bs
        s = lax.dot_general(k_ref[0:nk, :], q, _NT, preferred_element_type=F32)
        pieces = []
        if i > 0:
            gate = lax.dot_general(km, q, _NT, preferred_element_type=F32)
            rank = jnp.zeros((n_blocks, bs), jnp.int32)
            for m in range(i):
                gm = gate[m:m + 1, :]
                beats = jnp.logical_or(gm > gate, jnp.logical_and(gm == gate, blk_id > m))
                rank = rank + beats.astype(jnp.int32)
            sel = jnp.logical_and(blk_id < i, rank < MOBA_TOPK)
            for jb in range(i):
                keep = sel[jb:jb + 1, :]
                pieces.append(jnp.where(keep, s[jb * bs:(jb + 1) * bs, :], neg_inf))
        pieces.append(jnp.where(k_pos <= q_pos, s[i * bs:nk, :], neg_inf))
        m_col = functools.reduce(jnp.maximum, [jnp.max(p, axis=0, keepdims=True) for p in pieces])
        probs = [jnp.exp(p - m_col) for p in pieces]
        denom = sum(jnp.sum(p, axis=0, keepdims=True) for p in probs)
        p_all = probs[0] if i == 0 else jnp.concatenate(probs, axis=0)
        o_t = jnp.dot(vt_ref[:, 0:nk], p_all.astype(BF16), preferred_element_type=F32)
        o_ref[i * bs:(i + 1) * bs, :] = (o_t / denom).T.astype(o_ref.dtype)


def _attention(kq, vt, km, batch, seq, d):
    n_heads = d // HEAD_DIM
    n_blocks = seq // MOBA_BLOCK
    return pl.pallas_call(
        functools.partial(_attn_kernel, n_blocks=n_blocks),
        grid=(batch, n_heads),
        in_specs=[
            pl.BlockSpec((seq, HEAD_DIM), lambda b, h: (b, n_heads + h)),
            pl.BlockSpec((seq, HEAD_DIM), lambda b, h: (b, h)),
            pl.BlockSpec((HEAD_DIM, seq), lambda b, h: (h, b)),
            pl.BlockSpec((n_blocks, HEAD_DIM), lambda b, h: (b, h)),
        ],
        out_specs=pl.BlockSpec((seq, HEAD_DIM), lambda b, h: (b, h)),
        out_shape=jax.ShapeDtypeStruct((batch * seq, d), BF16),
        compiler_params=_params("parallel", "arbitrary"),
        name="moba_attn",
    )(kq, kq, vt, km)


def kernel(x, a_norm, a_w_in, a_ln_g, a_ln_b, a_w_s, a_b_s, a_w_out, kv_norm, w_k, w_v, k_norm,
           b_norm, b_w_q, b_q_norm, b_w_o, ffn_norm, ffn_w_gate, ffn_w_up, ffn_w_down):
    batch, seq, d = x.shape
    assert seq % MOBA_BLOCK == 0 and seq // MOBA_BLOCK - 1 >= MOBA_TOPK
    n_a = a_norm.shape[0]
    depth = ffn_norm.shape[0]
    bf = lambda w: w.astype(BF16)

    h = x.reshape(batch * seq, d)
    kv = None
    for layer in range(depth):
        if layer < n_a:
            i = layer
            u, v_ln = _sgu_in(h, a_norm[i], bf(a_w_in[i]), a_ln_g[i], a_ln_b[i])
            ug = _sgu_gate(u, v_ln, a_w_s[i], a_b_s[i])
            h = _mm_res(ug, bf(a_w_out[i]), h)
        else:
            i = layer - n_a
            assert depth - n_a == 1
            w_kvq = jnp.concatenate([bf(w_k), bf(w_v), bf(b_w_q[i])], axis=1)
            kq, vt, km = _kvq(h, kv_norm, b_norm[i], w_kvq, k_norm, b_q_norm[i])
            attn = _attention(kq, vt, km, batch, seq, d)
            h = _mm_res(attn, bf(b_w_o[i]), h)
        h = _ffn(h, ffn_norm[layer], bf(ffn_w_gate[layer]), bf(ffn_w_up[layer]), bf(ffn_w_down[layer]))
    return h.reshape(batch, seq, d)
```

```python
import functools

import jax
import jax.numpy as jnp
from jax import lax
from jax.experimental import pallas as pl
from jax.experimental.pallas import tpu as pltpu

F32 = jnp.float32
BF16 = jnp.bfloat16
EPS = 1e-6

SGU_CHUNK = 128
SGU_GROUPS = 16
HEAD_DIM = 128
MOBA_BLOCK = 256
MOBA_TOPK = 3

VMEM_LIMIT_BYTES = 56 * 1024 * 1024
ROW_TILE = 1024
ROW_BLOCK = 256
NORM_ROWS = 128


def _params(*sem):
    return pltpu.CompilerParams(dimension_semantics=sem, vmem_limit_bytes=VMEM_LIMIT_BYTES)


def _rows(n_rows, block, body):
    def step(r, carry):
        body(pl.multiple_of(r * block, block))
        return carry
    lax.fori_loop(0, n_rows // block, step, 0)


def _rms_scale(x):
    return x * lax.rsqrt(jnp.mean(x * x, axis=-1, keepdims=True) + EPS)


def _sgu_in_kernel(x_ref, g_ref, w_ref, lng_ref, lnb_ref, u_ref, v_ref, h_scr, v_scr,
                   *, tm, tn, n_u, n_v):
    j = pl.program_id(1)

    @pl.when(j == 0)
    def _():
        def norm(r0):
            x = x_ref[pl.ds(r0, NORM_ROWS), :]
            h_scr[pl.ds(r0, NORM_ROWS), :] = (_rms_scale(x) * g_ref[...]).astype(BF16)
        _rows(tm, NORM_ROWS, norm)

    def gelu_mm(r0):
        z = jnp.dot(h_scr[pl.ds(r0, ROW_BLOCK), :], w_ref[...].astype(BF16),
                    preferred_element_type=F32)
        return jax.nn.gelu(z, approximate=True)

    @pl.when(j < n_u)
    def _():
        def body(r0):
            u_ref[pl.ds(r0, ROW_BLOCK), :] = gelu_mm(r0).astype(u_ref.dtype)
        _rows(tm, ROW_BLOCK, body)

    @pl.when(j >= n_u)
    def _():
        def body(r0):
            v_scr[j - n_u, pl.ds(r0, ROW_BLOCK), :] = gelu_mm(r0)
        _rows(tm, ROW_BLOCK, body)

    @pl.when(j == n_u + n_v - 1)
    def _():
        d = n_v * tn

        def ln(r0):
            parts = [v_scr[t, pl.ds(r0, NORM_ROWS), :] for t in range(n_v)]
            mu = sum(jnp.sum(p, axis=-1, keepdims=True) for p in parts) / d
            cen = [p - mu for p in parts]
            var = sum(jnp.sum(c * c, axis=-1, keepdims=True) for c in cen) / d
            inv = lax.rsqrt(var + EPS)
            for t in range(n_v):
                cols = slice(t * tn, (t + 1) * tn)
                y = cen[t] * inv * lng_ref[:, cols] + lnb_ref[:, cols]
                v_ref[pl.ds(r0, NORM_ROWS), cols] = y.astype(v_ref.dtype)
        _rows(tm, NORM_ROWS, ln)


def _sgu_in(x2d, norm_g, w_in, layer, ln_g, ln_b):
    t, d = x2d.shape
    d_sgu = w_in.shape[2] // 2
    tm, tn = ROW_TILE, 512
    n_u = n_v = d_sgu // tn
    kern = functools.partial(_sgu_in_kernel, tm=tm, tn=tn, n_u=n_u, n_v=n_v)
    return pl.pallas_call(
        kern,
        grid=(t // tm, n_u + n_v),
        in_specs=[
            pl.BlockSpec((tm, d), lambda i, j: (i, 0)),
            pl.BlockSpec((1, d), lambda i, j: (0, 0)),
            pl.BlockSpec((None, d, tn), lambda i, j: (layer, 0, j)),
            pl.BlockSpec((1, d_sgu), lambda i, j: (0, 0)),
            pl.BlockSpec((1, d_sgu), lambda i, j: (0, 0)),
        ],
        out_specs=[
            pl.BlockSpec((tm, tn), lambda i, j: (i, jnp.minimum(j, n_u - 1))),
            pl.BlockSpec((tm, d_sgu), lambda i, j: (i, 0)),
        ],
        out_shape=[jax.ShapeDtypeStruct((t, d_sgu), BF16),
                   jax.ShapeDtypeStruct((t, d_sgu), BF16)],
        scratch_shapes=[pltpu.VMEM((tm, d), BF16), pltpu.VMEM((n_v, tm, tn), F32)],
        compiler_params=_params("parallel", "arbitrary"),
        name="sgu_in",
    )(x2d, norm_g.reshape(1, d), w_in, ln_g.reshape(1, d_sgu), ln_b.reshape(1, d_sgu))


def _sgu_gate_kernel(u_ref, v_ref, ws_ref, bs_ref, o_ref, wm_scr, *, tm):
    c = SGU_CHUNK

    @pl.when(pl.program_id(0) == 0)
    def _():
        t_idx = lax.broadcasted_iota(jnp.int32, (c, c), 0)
        s_idx = lax.broadcasted_iota(jnp.int32, (c, c), 1)
        for g in range(SGU_GROUPS):
            wm_scr[g] = jnp.where(s_idx <= t_idx, ws_ref[g], 0.0).astype(BF16)

    def chunk(r0):
        for g in range(SGU_GROUPS):
            cols = slice(g * c, (g + 1) * c)
            mixed = jnp.dot(wm_scr[g], v_ref[pl.ds(r0, c), cols], preferred_element_type=F32)
            gate = mixed + bs_ref[g]
            o_ref[pl.ds(r0, c), cols] = (u_ref[pl.ds(r0, c), cols].astype(F32) * gate).astype(o_ref.dtype)
    _rows(tm, c, chunk)


def _sgu_gate(u, v_ln, w_s, b_s):
    t, d = u.shape
    tm = 512
    kern = functools.partial(_sgu_gate_kernel, tm=tm)
    return pl.pallas_call(
        kern,
        grid=(t // tm,),
        in_specs=[
            pl.BlockSpec((tm, d), lambda i: (i, 0)),
            pl.BlockSpec((tm, d), lambda i: (i, 0)),
            pl.BlockSpec(w_s.shape, lambda i: (0, 0, 0)),
            pl.BlockSpec((SGU_GROUPS, SGU_CHUNK, 1), lambda i: (0, 0, 0)),
        ],
        out_specs=pl.BlockSpec((tm, d), lambda i: (i, 0)),
        out_shape=jax.ShapeDtypeStruct((t, d), BF16),
        scratch_shapes=[pltpu.VMEM((SGU_GROUPS, SGU_CHUNK, SGU_CHUNK), BF16)],
        compiler_params=_params("arbitrary"),
        name="sgu_gate",
    )(u, v_ln, w_s, b_s.reshape(SGU_GROUPS, SGU_CHUNK, 1))


def _mm_res_kernel(a_ref, w_ref, x_ref, o_ref, *, tm):
    def body(r0):
        acc = jnp.dot(a_ref[pl.ds(r0, ROW_BLOCK), :], w_ref[...].astype(BF16),
                      preferred_element_type=F32)
        o_ref[pl.ds(r0, ROW_BLOCK), :] = x_ref[pl.ds(r0, ROW_BLOCK), :] + acc
    _rows(tm, ROW_BLOCK, body)


def _mm_res(a, w_stack, layer, x2d):
    t, k = a.shape
    n = w_stack.shape[2]
    tm, tn = ROW_TILE, 1024
    return pl.pallas_call(
        functools.partial(_mm_res_kernel, tm=tm),
        grid=(t // tm, n // tn),
        in_specs=[
            pl.BlockSpec((tm, k), lambda i, j: (i, 0)),
            pl.BlockSpec((None, k, tn), lambda i, j: (layer, 0, j)),
            pl.BlockSpec((tm, tn), lambda i, j: (i, j)),
        ],
        out_specs=pl.BlockSpec((tm, tn), lambda i, j: (i, j)),
        out_shape=jax.ShapeDtypeStruct((t, n), F32),
        compiler_params=_params("parallel", "arbitrary"),
        name="mm_res",
    )(a, w_stack, x2d)


def _ffn_kernel(x_ref, g_ref, wg_ref, wu_ref, wd_ref, o_ref, h_scr, *, tm):
    j = pl.program_id(1)

    @pl.when(j == 0)
    def _():
        def norm(r0):
            x = x_ref[pl.ds(r0, NORM_ROWS), :]
            h_scr[pl.ds(r0, NORM_ROWS), :] = (_rms_scale(x) * g_ref[...]).astype(BF16)
            o_ref[pl.ds(r0, NORM_ROWS), :] = x
        _rows(tm, NORM_ROWS, norm)

    def body(r0):
        hb = h_scr[pl.ds(r0, ROW_BLOCK), :]
        gate = jnp.dot(hb, wg_ref[...].astype(BF16), preferred_element_type=F32)
        up = jnp.dot(hb, wu_ref[...].astype(BF16), preferred_element_type=F32)
        act = (jax.nn.silu(gate) * up).astype(BF16)
        o_ref[pl.ds(r0, ROW_BLOCK), :] += jnp.dot(act, wd_ref[...].astype(BF16),
                                                  preferred_element_type=F32)
    _rows(tm, ROW_BLOCK, body)


def _ffn(x2d, norm_g, w_gate, w_up, w_down, layer):
    t, d = x2d.shape
    f = w_gate.shape[2]
    tm, tf = ROW_TILE, 512
    return pl.pallas_call(
        functools.partial(_ffn_kernel, tm=tm),
        grid=(t // tm, f // tf),
        in_specs=[
            pl.BlockSpec((tm, d), lambda i, j: (i, 0), pipeline_mode=pl.Buffered(1)),
            pl.BlockSpec((1, d), lambda i, j: (0, 0)),
            pl.BlockSpec((None, d, tf), lambda i, j: (layer, 0, j)),
            pl.BlockSpec((None, d, tf), lambda i, j: (layer, 0, j)),
            pl.BlockSpec((None, tf, d), lambda i, j: (layer, j, 0)),
        ],
        out_specs=pl.BlockSpec((tm, d), lambda i, j: (i, 0)),
        out_shape=jax.ShapeDtypeStruct((t, d), F32),
        scratch_shapes=[pltpu.VMEM((tm, d), BF16)],
        compiler_params=_params("parallel", "arbitrary"),
        name="ffn",
    )(x2d, norm_g.reshape(1, d), w_gate, w_up, w_down)


def _kvq_kernel(x_ref, gkv_ref, gq_ref, w_ref, nk_ref, nq_ref, kq_ref, vt_ref, km_ref,
                hkv_scr, hq_scr, *, tm, tn, n_proj, q_scale):
    j = pl.program_id(1)
    heads = tn // HEAD_DIM
    blocks = tm // MOBA_BLOCK

    @pl.when(j == 0)
    def _():
        def norm(r0):
            y = _rms_scale(x_ref[pl.ds(r0, NORM_ROWS), :])
            hkv_scr[pl.ds(r0, NORM_ROWS), :] = (y * gkv_ref[...]).astype(BF16)
            hq_scr[pl.ds(r0, NORM_ROWS), :] = (y * gq_ref[...]).astype(BF16)
        _rows(tm, NORM_ROWS, norm)

    def proj(h_scr, blk):
        rows = slice(blk * MOBA_BLOCK, (blk + 1) * MOBA_BLOCK)
        return rows, jnp.dot(h_scr[rows, :], w_ref[...].astype(BF16), preferred_element_type=F32)

    @pl.when(j < n_proj)
    def _():
        for blk in range(blocks):
            rows, y = proj(hkv_scr, blk)
            for hh in range(heads):
                cols = slice(hh * HEAD_DIM, (hh + 1) * HEAD_DIM)
                kh = _rms_scale(y[:, cols]) * nk_ref[...]
                kq_ref[rows, cols] = kh.astype(kq_ref.dtype)
                km_ref[0, blk:blk + 1, cols] = jnp.mean(kh, axis=0, keepdims=True)

    @pl.when(jnp.logical_and(j >= n_proj, j < 2 * n_proj))
    def _():
        for blk in range(blocks):
            rows, y = proj(hkv_scr, blk)
            vt_ref[:, rows] = y.T.astype(vt_ref.dtype)

    @pl.when(j >= 2 * n_proj)
    def _():
        for blk in range(blocks):
            rows, y = proj(hq_scr, blk)
            for hh in range(heads):
                cols = slice(hh * HEAD_DIM, (hh + 1) * HEAD_DIM)
                qh = _rms_scale(y[:, cols]) * nq_ref[...] * q_scale
                kq_ref[rows, cols] = qh.astype(kq_ref.dtype)


def _kvq(x2d, kv_norm, q_in_norm, w_kvq, k_norm, q_norm):
    t, d = x2d.shape
    tm, tn = ROW_TILE, 1024
    n_proj = d // tn
    blocks = tm // MOBA_BLOCK
    kern = functools.partial(_kvq_kernel, tm=tm, tn=tn, n_proj=n_proj, q_scale=HEAD_DIM ** -0.5)

    def kq_map(i, j):
        return i, jnp.where(j < n_proj, j, jnp.maximum(j - n_proj, n_proj - 1))

    kq, vt, km = pl.pallas_call(
        kern,
        grid=(t // tm, 3 * n_proj),
        in_specs=[
            pl.BlockSpec((tm, d), lambda i, j: (i, 0)),
            pl.BlockSpec((1, d), lambda i, j: (0, 0)),
            pl.BlockSpec((1, d), lambda i, j: (0, 0)),
            pl.BlockSpec((d, tn), lambda i, j: (0, j)),
            pl.BlockSpec((1, HEAD_DIM), lambda i, j: (0, 0)),
            pl.BlockSpec((1, HEAD_DIM), lambda i, j: (0, 0)),
        ],
        out_specs=[
            pl.BlockSpec((tm, tn), kq_map),
            pl.BlockSpec((tn, tm), lambda i, j: (jnp.clip(j - n_proj, 0, n_proj - 1), i)),
            pl.BlockSpec((1, blocks, tn), lambda i, j: (i, 0, jnp.minimum(j, n_proj - 1))),
        ],
        out_shape=[jax.ShapeDtypeStruct((t, 2 * d), BF16),
                   jax.ShapeDtypeStruct((d, t), BF16),
                   jax.ShapeDtypeStruct((t // tm, blocks, d), F32)],
        scratch_shapes=[pltpu.VMEM((tm, d), BF16), pltpu.VMEM((tm, d), BF16)],
        compiler_params=_params("parallel", "arbitrary"),
        name="kvq",
    )(x2d, kv_norm.reshape(1, d), q_in_norm.reshape(1, d), w_kvq,
      k_norm.reshape(1, HEAD_DIM), q_norm.reshape(1, HEAD_DIM))
    return kq, vt, km.reshape(t // MOBA_BLOCK, d)


_NT = (((1,), (1,)), ((), ()))


def _attn_kernel(q_ref, k_ref, vt_ref, km_ref, o_ref, *, n_blocks):
    bs = MOBA_BLOCK
    sub = 8
    grp = bs // sub
    km = km_ref[...].astype(BF16)
    blk_id = lax.broadcasted_iota(jnp.int32, (n_blocks, bs), 0)
    k_pos = lax.broadcasted_iota(jnp.int32, (bs, bs), 0)
    q_pos = lax.broadcasted_iota(jnp.int32, (bs, bs), 1)
    neg_inf = -jnp.inf
    causal = jnp.where(k_pos <= q_pos, 0.0, neg_inf).reshape(grp, sub, bs)

    for i in range(n_blocks):
        q = q_ref[i * bs:(i + 1) * bs, :]
        nk = (i + 1) * bs
        s = lax.dot_general(k_ref[0:nk, :], q, _NT, preferred_element_type=F32)
        s = s.reshape(nk // sub, sub, bs)
        pieces = []
        if i > 0:
            gate = lax.dot_general(km, q, _NT, preferred_element_type=F32)
            rank = jnp.zeros((n_blocks, bs), F32)
            for m in range(i):
                gm = jnp.broadcast_to(gate[m:m + 1, :], (n_blocks, bs))
                beats = jnp.logical_or(gm > gate, jnp.logical_and(gm == gate, blk_id > m))
                rank = rank + jnp.where(beats, 1.0, 0.0)
            sel = jnp.logical_and(blk_id < i, rank < MOBA_TOPK)
            bias = jnp.where(sel, 0.0, neg_inf)
            for jb in range(i):
                b8 = jnp.broadcast_to(bias[jb:jb + 1, :], (sub, bs))
                pieces.append(s[jb * grp:(jb + 1) * grp] + b8[None])
        pieces.append(s[i * grp:(i + 1) * grp] + causal)
        m8 = functools.reduce(jnp.maximum, [jnp.max(p, axis=0) for p in pieces])
        m_b = jnp.broadcast_to(jnp.max(m8, axis=0, keepdims=True), (sub, bs))
        probs = [jnp.exp(p - m_b[None]) for p in pieces]
        l8 = sum(jnp.sum(p, axis=0) for p in probs)
        denom = jnp.sum(l8, axis=0, keepdims=True)
        p_all = probs[0] if i == 0 else jnp.concatenate(probs, axis=0)
        p_all = p_all.reshape(nk, bs).astype(BF16)
        o_t = jnp.dot(vt_ref[:, 0:nk], p_all, preferred_element_type=F32)
        o_ref[i * bs:(i + 1) * bs, :] = (o_t / denom).T.astype(o_ref.dtype)


def _attention(kq, vt, km, batch, seq, d):
    n_heads = d // HEAD_DIM
    n_blocks = seq // MOBA_BLOCK
    return pl.pallas_call(
        functools.partial(_attn_kernel, n_blocks=n_blocks),
        grid=(batch, n_heads),
        in_specs=[
            pl.BlockSpec((seq, HEAD_DIM), lambda b, h: (b, n_heads + h)),
            pl.BlockSpec((seq, HEAD_DIM), lambda b, h: (b, h)),
            pl.BlockSpec((HEAD_DIM, seq), lambda b, h: (h, b)),
            pl.BlockSpec((n_blocks, HEAD_DIM), lambda b, h: (b, h)),
        ],
        out_specs=pl.BlockSpec((seq, HEAD_DIM), lambda b, h: (b, h)),
        out_shape=jax.ShapeDtypeStruct((batch * seq, d), BF16),
        compiler_params=_params("parallel", "arbitrary"),
        name="moba_attn",
    )(kq, kq, vt, km)


def kernel(x, a_norm, a_w_in, a_ln_g, a_ln_b, a_w_s, a_b_s, a_w_out, kv_norm, w_k, w_v, k_norm,
           b_norm, b_w_q, b_q_norm, b_w_o, ffn_norm, ffn_w_gate, ffn_w_up, ffn_w_down):
    batch, seq, d = x.shape
    assert seq % MOBA_BLOCK == 0 and seq // MOBA_BLOCK - 1 >= MOBA_TOPK
    n_a = a_norm.shape[0]
    depth = ffn_norm.shape[0]
    bf = lambda w: w.astype(BF16)

    h = x.reshape(batch * seq, d)
    kv = None
    for layer in range(depth):
        if layer < n_a:
            i = layer
            u, v_ln = _sgu_in(h, a_norm[i], a_w_in, i, a_ln_g[i], a_ln_b[i])
            ug = _sgu_gate(u, v_ln, a_w_s[i], a_b_s[i])
            h = _mm_res(ug, a_w_out, i, h)
        else:
            i = layer - n_a
            assert depth - n_a == 1
            w_kvq = jnp.concatenate([bf(w_k), bf(w_v), bf(b_w_q[i])], axis=1)
            kq, vt, km = _kvq(h, kv_norm, b_norm[i], w_kvq, k_norm, b_q_norm[i])
            attn = _attention(kq, vt, km, batch, seq, d)
            h = _mm_res(attn, b_w_o, i, h)
        h = _ffn(h, ffn_norm[layer], ffn_w_gate, ffn_w_up, ffn_w_down, layer)
    return h.reshape(batch, seq, d)
```

```python
import functools

import jax
import jax.numpy as jnp
from jax import lax
from jax.experimental import pallas as pl
from jax.experimental.pallas import tpu as pltpu

F32 = jnp.float32
BF16 = jnp.bfloat16
EPS = 1e-6

SGU_CHUNK = 128
SGU_GROUPS = 16
HEAD_DIM = 128
MOBA_BLOCK = 256
MOBA_TOPK = 3

VMEM_LIMIT_BYTES = 56 * 1024 * 1024
ROW_TILE = 1024
ROW_BLOCK = 256
NORM_ROWS = 128
FFN_ROW_TILE = 2048
FFN_ROW_BLOCK = 512


def _params(*sem):
    return pltpu.CompilerParams(dimension_semantics=sem, vmem_limit_bytes=VMEM_LIMIT_BYTES)


def _rows(n_rows, block, body):
    def step(r, carry):
        body(pl.multiple_of(r * block, block))
        return carry
    lax.fori_loop(0, n_rows // block, step, 0)


def _rms_scale(x):
    return x * lax.rsqrt(jnp.mean(x * x, axis=-1, keepdims=True) + EPS)


def _sgu_in_kernel(x_ref, g_ref, w_ref, lng_ref, lnb_ref, u_ref, v_ref, h_scr, v_scr,
                   *, tm, tn, n_u, n_v):
    j = pl.program_id(1)

    @pl.when(j == 0)
    def _():
        def norm(r0):
            x = x_ref[pl.ds(r0, NORM_ROWS), :]
            h_scr[pl.ds(r0, NORM_ROWS), :] = (_rms_scale(x) * g_ref[...]).astype(BF16)
        _rows(tm, NORM_ROWS, norm)

    def gelu_mm(r0):
        z = jnp.dot(h_scr[pl.ds(r0, ROW_BLOCK), :], w_ref[...].astype(BF16),
                    preferred_element_type=F32)
        return jax.nn.gelu(z, approximate=True)

    @pl.when(j < n_u)
    def _():
        def body(r0):
            u_ref[pl.ds(r0, ROW_BLOCK), :] = gelu_mm(r0).astype(u_ref.dtype)
        _rows(tm, ROW_BLOCK, body)

    @pl.when(j >= n_u)
    def _():
        def body(r0):
            v_scr[j - n_u, pl.ds(r0, ROW_BLOCK), :] = gelu_mm(r0)
        _rows(tm, ROW_BLOCK, body)

    @pl.when(j == n_u + n_v - 1)
    def _():
        d = n_v * tn

        def ln(r0):
            parts = [v_scr[t, pl.ds(r0, NORM_ROWS), :] for t in range(n_v)]
            mu = sum(jnp.sum(p, axis=-1, keepdims=True) for p in parts) / d
            cen = [p - mu for p in parts]
            var = sum(jnp.sum(c * c, axis=-1, keepdims=True) for c in cen) / d
            inv = lax.rsqrt(var + EPS)
            for t in range(n_v):
                cols = slice(t * tn, (t + 1) * tn)
                y = cen[t] * inv * lng_ref[:, cols] + lnb_ref[:, cols]
                v_ref[pl.ds(r0, NORM_ROWS), cols] = y.astype(v_ref.dtype)
        _rows(tm, NORM_ROWS, ln)


def _sgu_in(x2d, norm_g, w_in, layer, ln_g, ln_b):
    t, d = x2d.shape
    d_sgu = w_in.shape[2] // 2
    tm, tn = ROW_TILE, 512
    n_u = n_v = d_sgu // tn
    kern = functools.partial(_sgu_in_kernel, tm=tm, tn=tn, n_u=n_u, n_v=n_v)
    return pl.pallas_call(
        kern,
        grid=(t // tm, n_u + n_v),
        in_specs=[
            pl.BlockSpec((tm, d), lambda i, j: (i, 0)),
            pl.BlockSpec((1, d), lambda i, j: (0, 0)),
            pl.BlockSpec((None, d, tn), lambda i, j: (layer, 0, j)),
            pl.BlockSpec((1, d_sgu), lambda i, j: (0, 0)),
            pl.BlockSpec((1, d_sgu), lambda i, j: (0, 0)),
        ],
        out_specs=[
            pl.BlockSpec((tm, tn), lambda i, j: (i, jnp.minimum(j, n_u - 1))),
            pl.BlockSpec((tm, d_sgu), lambda i, j: (i, 0)),
        ],
        out_shape=[jax.ShapeDtypeStruct((t, d_sgu), BF16),
                   jax.ShapeDtypeStruct((t, d_sgu), BF16)],
        scratch_shapes=[pltpu.VMEM((tm, d), BF16), pltpu.VMEM((n_v, tm, tn), F32)],
        compiler_params=_params("parallel", "arbitrary"),
        name="sgu_in",
    )(x2d, norm_g.reshape(1, d), w_in, ln_g.reshape(1, d_sgu), ln_b.reshape(1, d_sgu))


def _sgu_gate_kernel(u_ref, v_ref, ws_ref, bs_ref, o_ref, wm_scr, *, tm):
    c = SGU_CHUNK

    @pl.when(pl.program_id(0) == 0)
    def _():
        t_idx = lax.broadcasted_iota(jnp.int32, (c, c), 0)
        s_idx = lax.broadcasted_iota(jnp.int32, (c, c), 1)
        for g in range(SGU_GROUPS):
            wm_scr[g] = jnp.where(s_idx <= t_idx, ws_ref[g], 0.0).astype(BF16)

    def chunk(r0):
        for g in range(SGU_GROUPS):
            cols = slice(g * c, (g + 1) * c)
            mixed = jnp.dot(wm_scr[g], v_ref[pl.ds(r0, c), cols], preferred_element_type=F32)
            gate = mixed + bs_ref[g]
            o_ref[pl.ds(r0, c), cols] = (u_ref[pl.ds(r0, c), cols].astype(F32) * gate).astype(o_ref.dtype)
    _rows(tm, c, chunk)


def _sgu_gate(u, v_ln, w_s, b_s):
    t, d = u.shape
    tm = 512
    kern = functools.partial(_sgu_gate_kernel, tm=tm)
    return pl.pallas_call(
        kern,
        grid=(t // tm,),
        in_specs=[
            pl.BlockSpec((tm, d), lambda i: (i, 0)),
            pl.BlockSpec((tm, d), lambda i: (i, 0)),
            pl.BlockSpec(w_s.shape, lambda i: (0, 0, 0)),
            pl.BlockSpec((SGU_GROUPS, SGU_CHUNK, 1), lambda i: (0, 0, 0)),
        ],
        out_specs=pl.BlockSpec((tm, d), lambda i: (i, 0)),
        out_shape=jax.ShapeDtypeStruct((t, d), BF16),
        scratch_shapes=[pltpu.VMEM((SGU_GROUPS, SGU_CHUNK, SGU_CHUNK), BF16)],
        compiler_params=_params("arbitrary"),
        name="sgu_gate",
    )(u, v_ln, w_s, b_s.reshape(SGU_GROUPS, SGU_CHUNK, 1))


def _mm_res_kernel(a_ref, w_ref, x_ref, o_ref, *, tm, rb):
    def body(r0):
        acc = jnp.dot(a_ref[pl.ds(r0, rb), :], w_ref[...].astype(BF16),
                      preferred_element_type=F32)
        o_ref[pl.ds(r0, rb), :] = x_ref[pl.ds(r0, rb), :] + acc
    _rows(tm, rb, body)


def _mm_res(a, w_stack, layer, x2d):
    t, k = a.shape
    n = w_stack.shape[2]
    tm, tn = FFN_ROW_TILE, 512
    return pl.pallas_call(
        functools.partial(_mm_res_kernel, tm=tm, rb=FFN_ROW_BLOCK),
        grid=(t // tm, n // tn),
        in_specs=[
            pl.BlockSpec((tm, k), lambda i, j: (i, 0)),
            pl.BlockSpec((None, k, tn), lambda i, j: (layer, 0, j)),
            pl.BlockSpec((tm, tn), lambda i, j: (i, j)),
        ],
        out_specs=pl.BlockSpec((tm, tn), lambda i, j: (i, j)),
        out_shape=jax.ShapeDtypeStruct((t, n), F32),
        compiler_params=_params("parallel", "arbitrary"),
        name="mm_res",
    )(a, w_stack, x2d)


def _ffn_kernel(x_hbm, g_ref, wg_ref, wu_ref, wd_ref, o_hbm, acc, h_scr, in_sem, out_sem,
                *, tm, rb):
    i = pl.program_id(0)
    j = pl.program_id(1)
    n_i = pl.num_programs(0)
    n_j = pl.num_programs(1)
    n_blk = tm // rb

    def in_copy(c):
        return pltpu.make_async_copy(x_hbm.at[pl.ds(i * tm + c * rb, rb), :],
                                     acc.at[pl.ds(c * rb, rb), :], in_sem.at[c])

    def out_copy(tile, c):
        start = pl.multiple_of(c * rb, rb)
        return pltpu.make_async_copy(acc.at[pl.ds(start, rb), :],
                                     o_hbm.at[pl.ds(tile * tm + start, rb), :], out_sem.at[c])

    @pl.when(j == 0)
    def _():
        for c in range(n_blk):
            @pl.when(i > 0)
            def _():
                out_copy(i - 1, c).wait()
            in_copy(c).start()
        for c in range(n_blk):
            in_copy(c).wait()

            def norm(r0):
                rows = pl.ds(pl.multiple_of(c * rb + r0, NORM_ROWS), NORM_ROWS)
                h_scr[rows, :] = (_rms_scale(acc[rows, :]) * g_ref[...]).astype(BF16)
            _rows(rb, NORM_ROWS, norm)

    def body(r, carry):
        rows = pl.ds(pl.multiple_of(r * rb, rb), rb)
        hb = h_scr[rows, :]
        gate = jnp.dot(hb, wg_ref[...].astype(BF16), preferred_element_type=F32)
        up = jnp.dot(hb, wu_ref[...].astype(BF16), preferred_element_type=F32)
        act = (jax.nn.silu(gate) * up).astype(BF16)
        acc[rows, :] += jnp.dot(act, wd_ref[...].astype(BF16), preferred_element_type=F32)

        @pl.when(j == n_j - 1)
        def _():
            out_copy(i, r).start()
        return carry
    lax.fori_loop(0, n_blk, body, 0)

    @pl.when(jnp.logical_and(i == n_i - 1, j == n_j - 1))
    def _():
        for c in range(n_blk):
            out_copy(i, c).wait()


def _ffn(x2d, norm_g, w_gate, w_up, w_down, layer):
    t, d = x2d.shape
    f = w_gate.shape[2]
    tm, tf, rb = FFN_ROW_TILE, 512, FFN_ROW_BLOCK
    return pl.pallas_call(
        functools.partial(_ffn_kernel, tm=tm, rb=rb),
        grid=(t // tm, f // tf),
        in_specs=[
            pl.BlockSpec(memory_space=pl.ANY),
            pl.BlockSpec((1, d), lambda i, j: (0, 0)),
            pl.BlockSpec((None, d, tf), lambda i, j: (layer, 0, j)),
            pl.BlockSpec((None, d, tf), lambda i, j: (layer, 0, j)),
            pl.BlockSpec((None, tf, d), lambda i, j: (layer, j, 0)),
        ],
        out_specs=pl.BlockSpec(memory_space=pl.ANY),
        out_shape=jax.ShapeDtypeStruct((t, d), F32),
        scratch_shapes=[pltpu.VMEM((tm, d), F32), pltpu.VMEM((tm, d), BF16),
                        pltpu.SemaphoreType.DMA((tm // rb,)), pltpu.SemaphoreType.DMA((tm // rb,))],
        compiler_params=_params("arbitrary", "arbitrary"),
        name="ffn",
    )(x2d, norm_g.reshape(1, d), w_gate, w_up, w_down)


def _kvq_kernel(x_ref, gkv_ref, gq_ref, w_ref, nk_ref, nq_ref, kq_ref, vt_ref, km_ref,
                hkv_scr, hq_scr, *, tm, tn, n_proj, q_scale):
    j = pl.program_id(1)
    heads = tn // HEAD_DIM
    blocks = tm // MOBA_BLOCK

    @pl.when(j == 0)
    def _():
        def norm(r0):
            y = _rms_scale(x_ref[pl.ds(r0, NORM_ROWS), :])
            hkv_scr[pl.ds(r0, NORM_ROWS), :] = (y * gkv_ref[...]).astype(BF16)
            hq_scr[pl.ds(r0, NORM_ROWS), :] = (y * gq_ref[...]).astype(BF16)
        _rows(tm, NORM_ROWS, norm)

    def proj(h_scr, blk):
        rows = slice(blk * MOBA_BLOCK, (blk + 1) * MOBA_BLOCK)
        return rows, jnp.dot(h_scr[rows, :], w_ref[...].astype(BF16), preferred_element_type=F32)

    @pl.when(j < n_proj)
    def _():
        for blk in range(blocks):
            rows, y = proj(hkv_scr, blk)
            for hh in range(heads):
                cols = slice(hh * HEAD_DIM, (hh + 1) * HEAD_DIM)
                kh = _rms_scale(y[:, cols]) * nk_ref[...]
                kq_ref[rows, cols] = kh.astype(kq_ref.dtype)
                km_ref[0, blk:blk + 1, cols] = jnp.mean(kh, axis=0, keepdims=True)

    @pl.when(jnp.logical_and(j >= n_proj, j < 2 * n_proj))
    def _():
        for blk in range(blocks):
            rows, y = proj(hkv_scr, blk)
            vt_ref[:, rows] = y.T.astype(vt_ref.dtype)

    @pl.when(j >= 2 * n_proj)
    def _():
        for blk in range(blocks):
            rows, y = proj(hq_scr, blk)
            for hh in range(heads):
                cols = slice(hh * HEAD_DIM, (hh + 1) * HEAD_DIM)
                qh = _rms_scale(y[:, cols]) * nq_ref[...] * q_scale
                kq_ref[rows, cols] = qh.astype(kq_ref.dtype)


def _kvq(x2d, kv_norm, q_in_norm, w_kvq, k_norm, q_norm):
    t, d = x2d.shape
    tm, tn = ROW_TILE, 1024
    n_proj = d // tn
    blocks = tm // MOBA_BLOCK
    kern = functools.partial(_kvq_kernel, tm=tm, tn=tn, n_proj=n_proj, q_scale=HEAD_DIM ** -0.5)

    def kq_map(i, j):
        return i, jnp.where(j < n_proj, j, jnp.maximum(j - n_proj, n_proj - 1))

    kq, vt, km = pl.pallas_call(
        kern,
        grid=(t // tm, 3 * n_proj),
        in_specs=[
            pl.BlockSpec((tm, d), lambda i, j: (i, 0)),
            pl.BlockSpec((1, d), lambda i, j: (0, 0)),
            pl.BlockSpec((1, d), lambda i, j: (0, 0)),
            pl.BlockSpec((d, tn), lambda i, j: (0, j)),
            pl.BlockSpec((1, HEAD_DIM), lambda i, j: (0, 0)),
            pl.BlockSpec((1, HEAD_DIM), lambda i, j: (0, 0)),
        ],
        out_specs=[
            pl.BlockSpec((tm, tn), kq_map),
            pl.BlockSpec((tn, tm), lambda i, j: (jnp.clip(j - n_proj, 0, n_proj - 1), i)),
            pl.BlockSpec((1, blocks, tn), lambda i, j: (i, 0, jnp.minimum(j, n_proj - 1))),
        ],
        out_shape=[jax.ShapeDtypeStruct((t, 2 * d), BF16),
                   jax.ShapeDtypeStruct((d, t), BF16),
                   jax.ShapeDtypeStruct((t // tm, blocks, d), F32)],
        scratch_shapes=[pltpu.VMEM((tm, d), BF16), pltpu.VMEM((tm, d), BF16)],
        compiler_params=_params("parallel", "arbitrary"),
        name="kvq",
    )(x2d, kv_norm.reshape(1, d), q_in_norm.reshape(1, d), w_kvq,
      k_norm.reshape(1, HEAD_DIM), q_norm.reshape(1, HEAD_DIM))
    return kq, vt, km.reshape(t // MOBA_BLOCK, d)


_NT = (((1,), (1,)), ((), ()))


def _attn_kernel(q_ref, k_ref, vt_ref, km_ref, o_ref, *, n_blocks):
    bs = MOBA_BLOCK
    sub = 8
    grp = bs // sub
    km = km_ref[...].astype(BF16)
    blk_id = lax.broadcasted_iota(jnp.int32, (n_blocks, bs), 0)
    k_pos = lax.broadcasted_iota(jnp.int32, (bs, bs), 0)
    q_pos = lax.broadcasted_iota(jnp.int32, (bs, bs), 1)
    neg_inf = -jnp.inf
    causal = jnp.where(k_pos <= q_pos, 0.0, neg_inf).reshape(grp, sub, bs)

    for i in range(n_blocks):
        q = q_ref[i * bs:(i + 1) * bs, :]
        nk = (i + 1) * bs
        s = lax.dot_general(k_ref[0:nk, :], q, _NT, preferred_element_type=F32)
        s = s.reshape(nk // sub, sub, bs)
        pieces = []
        if i > 0:
            gate = lax.dot_general(km, q, _NT, preferred_element_type=F32)
            rank = jnp.zeros((n_blocks, bs), F32)
            for m in range(i):
                gm = jnp.broadcast_to(gate[m:m + 1, :], (n_blocks, bs))
                beats = jnp.logical_or(gm > gate, jnp.logical_and(gm == gate, blk_id > m))
                rank = rank + jnp.where(beats, 1.0, 0.0)
            sel = jnp.logical_and(blk_id < i, rank < MOBA_TOPK)
            bias = jnp.where(sel, 0.0, neg_inf)
            for jb in range(i):
                b8 = jnp.broadcast_to(bias[jb:jb + 1, :], (sub, bs))
                pieces.append(s[jb * grp:(jb + 1) * grp] + b8[None])
        pieces.append(s[i * grp:(i + 1) * grp] + causal)
        m8 = functools.reduce(jnp.maximum, [jnp.max(p, axis=0) for p in pieces])
        m_b = jnp.broadcast_to(jnp.max(m8, axis=0, keepdims=True), (sub, bs))
        probs = [jnp.exp(p - m_b[None]) for p in pieces]
        l8 = sum(jnp.sum(p, axis=0) for p in probs)
        denom = jnp.sum(l8, axis=0, keepdims=True)
        p_all = probs[0] if i == 0 else jnp.concatenate(probs, axis=0)
        p_all = p_all.reshape(nk, bs).astype(BF16)
        o_t = jnp.dot(vt_ref[:, 0:nk], p_all, preferred_element_type=F32)
        o_ref[i * bs:(i + 1) * bs, :] = (o_t / denom).T.astype(o_ref.dtype)


def _attention(kq, vt, km, batch, seq, d):
    n_heads = d // HEAD_DIM
    n_blocks = seq // MOBA_BLOCK
    return pl.pallas_call(
        functools.partial(_attn_kernel, n_blocks=n_blocks),
        grid=(batch, n_heads),
        in_specs=[
            pl.BlockSpec((seq, HEAD_DIM), lambda b, h: (b, n_heads + h)),
            pl.BlockSpec((seq, HEAD_DIM), lambda b, h: (b, h)),
            pl.BlockSpec((HEAD_DIM, seq), lambda b, h: (h, b)),
            pl.BlockSpec((n_blocks, HEAD_DIM), lambda b, h: (b, h)),
        ],
        out_specs=pl.BlockSpec((seq, HEAD_DIM), lambda b, h: (b, h)),
        out_shape=jax.ShapeDtypeStruct((batch * seq, d), BF16),
        compiler_params=_params("parallel", "arbitrary"),
        name="moba_attn",
    )(kq, kq, vt, km)


def kernel(x, a_norm, a_w_in, a_ln_g, a_ln_b, a_w_s, a_b_s, a_w_out, kv_norm, w_k, w_v, k_norm,
           b_norm, b_w_q, b_q_norm, b_w_o, ffn_norm, ffn_w_gate, ffn_w_up, ffn_w_down):
    batch, seq, d = x.shape
    assert seq % MOBA_BLOCK == 0 and seq // MOBA_BLOCK - 1 >= MOBA_TOPK
    n_a = a_norm.shape[0]
    depth = ffn_norm.shape[0]
    bf = lambda w: w.astype(BF16)

    h = x.reshape(batch * seq, d)
    kv = None
    for layer in range(depth):
        if layer < n_a:
            i = layer
            u, v_ln = _sgu_in(h, a_norm[i], a_w_in, i, a_ln_g[i], a_ln_b[i])
            ug = _sgu_gate(u, v_ln, a_w_s[i], a_b_s[i])
            h = _mm_res(ug, a_w_out, i, h)
        else:
            i = layer - n_a
            assert depth - n_a == 1
            w_kvq = jnp.concatenate([bf(w_k), bf(w_v), bf(b_w_q[i])], axis=1)
            kq, vt, km = _kvq(h, kv_norm, b_norm[i], w_kvq, k_norm, b_q_norm[i])
            attn = _attention(kq, vt, km, batch, seq, d)
            h = _mm_res(attn, b_w_o, i, h)
        h = _ffn(h, ffn_norm[layer], ffn_w_gate, ffn_w_up, ffn_w_down, layer)
    return h.reshape(batch, seq, d)
```

```python
import functools

import jax
import jax.numpy as jnp
from jax import lax
from jax.experimental import pallas as pl
from jax.experimental.pallas import tpu as pltpu

F32 = jnp.float32
BF16 = jnp.bfloat16
EPS = 1e-6

SGU_CHUNK = 128
SGU_GROUPS = 16
HEAD_DIM = 128
MOBA_BLOCK = 256
MOBA_TOPK = 3

VMEM_LIMIT_BYTES = 56 * 1024 * 1024
ROW_TILE = 1024
ROW_BLOCK = 256
NORM_ROWS = 128
FFN_ROW_TILE = 2048
FFN_ROW_BLOCK = 512


def _params(*sem):
    return pltpu.CompilerParams(dimension_semantics=sem, vmem_limit_bytes=VMEM_LIMIT_BYTES)


def _rows(n_rows, block, body, unroll=False):
    if unroll:
        for r in range(n_rows // block):
            body(r * block)
        return

    def step(r, carry):
        body(pl.multiple_of(r * block, block))
        return carry
    lax.fori_loop(0, n_rows // block, step, 0)


def _rms_scale(x):
    return x * lax.rsqrt(jnp.mean(x * x, axis=-1, keepdims=True) + EPS)


def _sgu_in_kernel(x_ref, g_ref, w_ref, lng_ref, lnb_ref, u_ref, v_ref, h_scr, v_scr,
                   *, tm, tn, n_u, n_v):
    j = pl.program_id(1)

    @pl.when(j == 0)
    def _():
        def norm(r0):
            x = x_ref[pl.ds(r0, NORM_ROWS), :]
            h_scr[pl.ds(r0, NORM_ROWS), :] = (_rms_scale(x) * g_ref[...]).astype(BF16)
        _rows(tm, NORM_ROWS, norm)

    def gelu_mm(r0):
        z = jnp.dot(h_scr[pl.ds(r0, ROW_BLOCK), :], w_ref[...].astype(BF16),
                    preferred_element_type=F32)
        return jax.nn.gelu(z, approximate=True)

    @pl.when(j < n_u)
    def _():
        def body(r0):
            u_ref[pl.ds(r0, ROW_BLOCK), :] = gelu_mm(r0).astype(u_ref.dtype)
        _rows(tm, ROW_BLOCK, body, unroll=True)

    @pl.when(j >= n_u)
    def _():
        def body(r0):
            v_scr[j - n_u, pl.ds(r0, ROW_BLOCK), :] = gelu_mm(r0)
        _rows(tm, ROW_BLOCK, body, unroll=True)

    @pl.when(j == n_u + n_v - 1)
    def _():
        d = n_v * tn

        def ln(r0):
            parts = [v_scr[t, pl.ds(r0, NORM_ROWS), :] for t in range(n_v)]
            mu = sum(jnp.sum(p, axis=-1, keepdims=True) for p in parts) / d
            cen = [p - mu for p in parts]
            var = sum(jnp.sum(c * c, axis=-1, keepdims=True) for c in cen) / d
            inv = lax.rsqrt(var + EPS)
            for t in range(n_v):
                cols = slice(t * tn, (t + 1) * tn)
                y = cen[t] * inv * lng_ref[:, cols] + lnb_ref[:, cols]
                v_ref[pl.ds(r0, NORM_ROWS), cols] = y.astype(v_ref.dtype)
        _rows(tm, NORM_ROWS, ln)


def _sgu_in(x2d, norm_g, w_in, layer, ln_g, ln_b):
    t, d = x2d.shape
    d_sgu = w_in.shape[2] // 2
    tm, tn = ROW_TILE, 512
    n_u = n_v = d_sgu // tn
    kern = functools.partial(_sgu_in_kernel, tm=tm, tn=tn, n_u=n_u, n_v=n_v)
    return pl.pallas_call(
        kern,
        grid=(t // tm, n_u + n_v),
        in_specs=[
            pl.BlockSpec((tm, d), lambda i, j: (i, 0)),
            pl.BlockSpec((1, d), lambda i, j: (0, 0)),
            pl.BlockSpec((None, d, tn), lambda i, j: (layer, 0, j)),
            pl.BlockSpec((1, d_sgu), lambda i, j: (0, 0)),
            pl.BlockSpec((1, d_sgu), lambda i, j: (0, 0)),
        ],
        out_specs=[
            pl.BlockSpec((tm, tn), lambda i, j: (i, jnp.minimum(j, n_u - 1))),
            pl.BlockSpec((tm, d_sgu), lambda i, j: (i, 0)),
        ],
        out_shape=[jax.ShapeDtypeStruct((t, d_sgu), BF16),
                   jax.ShapeDtypeStruct((t, d_sgu), BF16)],
        scratch_shapes=[pltpu.VMEM((tm, d), BF16), pltpu.VMEM((n_v, tm, tn), F32)],
        compiler_params=_params("parallel", "arbitrary"),
        name="sgu_in",
    )(x2d, norm_g.reshape(1, d), w_in, ln_g.reshape(1, d_sgu), ln_b.reshape(1, d_sgu))


def _sgu_gate_kernel(u_ref, v_ref, ws_ref, bs_ref, o_ref, wm_scr, *, tm):
    c = SGU_CHUNK

    @pl.when(pl.program_id(0) == 0)
    def _():
        t_idx = lax.broadcasted_iota(jnp.int32, (c, c), 0)
        s_idx = lax.broadcasted_iota(jnp.int32, (c, c), 1)
        for g in range(SGU_GROUPS):
            wm_scr[g] = jnp.where(s_idx <= t_idx, ws_ref[g], 0.0).astype(BF16)

    def chunk(r0):
        for g in range(SGU_GROUPS):
            cols = slice(g * c, (g + 1) * c)
            mixed = jnp.dot(wm_scr[g], v_ref[pl.ds(r0, c), cols], preferred_element_type=F32)
            gate = mixed + bs_ref[g]
            o_ref[pl.ds(r0, c), cols] = (u_ref[pl.ds(r0, c), cols].astype(F32) * gate).astype(o_ref.dtype)
    _rows(tm, c, chunk)


def _sgu_gate(u, v_ln, w_s, b_s):
    t, d = u.shape
    tm = 512
    kern = functools.partial(_sgu_gate_kernel, tm=tm)
    return pl.pallas_call(
        kern,
        grid=(t // tm,),
        in_specs=[
            pl.BlockSpec((tm, d), lambda i: (i, 0)),
            pl.BlockSpec((tm, d), lambda i: (i, 0)),
            pl.BlockSpec(w_s.shape, lambda i: (0, 0, 0)),
            pl.BlockSpec((SGU_GROUPS, SGU_CHUNK, 1), lambda i: (0, 0, 0)),
        ],
        out_specs=pl.BlockSpec((tm, d), lambda i: (i, 0)),
        out_shape=jax.ShapeDtypeStruct((t, d), BF16),
        scratch_shapes=[pltpu.VMEM((SGU_GROUPS, SGU_CHUNK, SGU_CHUNK), BF16)],
        compiler_params=_params("arbitrary"),
        name="sgu_gate",
    )(u, v_ln, w_s, b_s.reshape(SGU_GROUPS, SGU_CHUNK, 1))


def _mm_res_kernel(a_ref, w_ref, x_ref, o_ref, *, tm, rb):
    def body(r0):
        acc = jnp.dot(a_ref[pl.ds(r0, rb), :], w_ref[...].astype(BF16),
                      preferred_element_type=F32)
        o_ref[pl.ds(r0, rb), :] = x_ref[pl.ds(r0, rb), :] + acc
    _rows(tm, rb, body, unroll=True)


def _mm_res(a, w_stack, layer, x2d):
    t, k = a.shape
    n = w_stack.shape[2]
    tm, tn = FFN_ROW_TILE, 512
    return pl.pallas_call(
        functools.partial(_mm_res_kernel, tm=tm, rb=FFN_ROW_BLOCK),
        grid=(t // tm, n // tn),
        in_specs=[
            pl.BlockSpec((tm, k), lambda i, j: (i, 0)),
            pl.BlockSpec((None, k, tn), lambda i, j: (layer, 0, j)),
            pl.BlockSpec((tm, tn), lambda i, j: (i, j)),
        ],
        out_specs=pl.BlockSpec((tm, tn), lambda i, j: (i, j)),
        out_shape=jax.ShapeDtypeStruct((t, n), F32),
        compiler_params=_params("parallel", "arbitrary"),
        name="mm_res",
    )(a, w_stack, x2d)


def _ffn_kernel(x_hbm, g_ref, wg_ref, wu_ref, wd_ref, o_hbm, acc, h_scr, in_sem, out_sem,
                *, tm, rb):
    i = pl.program_id(0)
    j = pl.program_id(1)
    n_i = pl.num_programs(0)
    n_j = pl.num_programs(1)
    n_blk = tm // rb

    def in_copy(c):
        return pltpu.make_async_copy(x_hbm.at[pl.ds(i * tm + c * rb, rb), :],
                                     acc.at[pl.ds(c * rb, rb), :], in_sem.at[c])

    def out_copy(tile, c):
        return pltpu.make_async_copy(acc.at[pl.ds(c * rb, rb), :],
                                     o_hbm.at[pl.ds(tile * tm + c * rb, rb), :], out_sem.at[c])

    @pl.when(j == 0)
    def _():
        for c in range(n_blk):
            @pl.when(i > 0)
            def _():
                out_copy(i - 1, c).wait()
            in_copy(c).start()
        for c in range(n_blk):
            in_copy(c).wait()

            def norm(r0):
                rows = pl.ds(pl.multiple_of(c * rb + r0, NORM_ROWS), NORM_ROWS)
                h_scr[rows, :] = (_rms_scale(acc[rows, :]) * g_ref[...]).astype(BF16)
            _rows(rb, NORM_ROWS, norm)

    def trips(write_back):
        for r in range(n_blk):
            rows = pl.ds(r * rb, rb)
            hb = h_scr[rows, :]
            gate = jnp.dot(hb, wg_ref[...].astype(BF16), preferred_element_type=F32)
            up = jnp.dot(hb, wu_ref[...].astype(BF16), preferred_element_type=F32)
            act = (jax.nn.silu(gate) * up).astype(BF16)
            acc[rows, :] += jnp.dot(act, wd_ref[...].astype(BF16), preferred_element_type=F32)
            if write_back:
                out_copy(i, r).start()

    @pl.when(j < n_j - 1)
    def _():
        trips(False)

    @pl.when(j == n_j - 1)
    def _():
        trips(True)

    @pl.when(jnp.logical_and(i == n_i - 1, j == n_j - 1))
    def _():
        for c in range(n_blk):
            out_copy(i, c).wait()


def _ffn(x2d, norm_g, w_gate, w_up, w_down, layer):
    t, d = x2d.shape
    f = w_gate.shape[2]
    tm, tf, rb = FFN_ROW_TILE, 512, FFN_ROW_BLOCK
    return pl.pallas_call(
        functools.partial(_ffn_kernel, tm=tm, rb=rb),
        grid=(t // tm, f // tf),
        in_specs=[
            pl.BlockSpec(memory_space=pl.ANY),
            pl.BlockSpec((1, d), lambda i, j: (0, 0)),
            pl.BlockSpec((None, d, tf), lambda i, j: (layer, 0, j)),
            pl.BlockSpec((None, d, tf), lambda i, j: (layer, 0, j)),
            pl.BlockSpec((None, tf, d), lambda i, j: (layer, j, 0)),
        ],
        out_specs=pl.BlockSpec(memory_space=pl.ANY),
        out_shape=jax.ShapeDtypeStruct((t, d), F32),
        scratch_shapes=[pltpu.VMEM((tm, d), F32), pltpu.VMEM((tm, d), BF16),
                        pltpu.SemaphoreType.DMA((tm // rb,)), pltpu.SemaphoreType.DMA((tm // rb,))],
        compiler_params=_params("arbitrary", "arbitrary"),
        name="ffn",
    )(x2d, norm_g.reshape(1, d), w_gate, w_up, w_down)


def _kvq_kernel(x_ref, gkv_ref, gq_ref, w_ref, nk_ref, nq_ref, kq_ref, vt_ref, km_ref,
                hkv_scr, hq_scr, *, tm, tn, n_proj, q_scale):
    j = pl.program_id(1)
    heads = tn // HEAD_DIM
    blocks = tm // MOBA_BLOCK

    @pl.when(j == 0)
    def _():
        def norm(r0):
            y = _rms_scale(x_ref[pl.ds(r0, NORM_ROWS), :])
            hkv_scr[pl.ds(r0, NORM_ROWS), :] = (y * gkv_ref[...]).astype(BF16)
            hq_scr[pl.ds(r0, NORM_ROWS), :] = (y * gq_ref[...]).astype(BF16)
        _rows(tm, NORM_ROWS, norm)

    def proj(h_scr, blk):
        rows = slice(blk * MOBA_BLOCK, (blk + 1) * MOBA_BLOCK)
        return rows, jnp.dot(h_scr[rows, :], w_ref[...].astype(BF16), preferred_element_type=F32)

    @pl.when(j < n_proj)
    def _():
        for blk in range(blocks):
            rows, y = proj(hkv_scr, blk)
            for hh in range(heads):
                cols = slice(hh * HEAD_DIM, (hh + 1) * HEAD_DIM)
                kh = _rms_scale(y[:, cols]) * nk_ref[...]
                kq_ref[rows, cols] = kh.astype(kq_ref.dtype)
                km_ref[0, blk:blk + 1, cols] = jnp.mean(kh, axis=0, keepdims=True)

    @pl.when(jnp.logical_and(j >= n_proj, j < 2 * n_proj))
    def _():
        for blk in range(blocks):
            rows, y = proj(hkv_scr, blk)
            vt_ref[:, rows] = y.T.astype(vt_ref.dtype)

    @pl.when(j >= 2 * n_proj)
    def _():
        for blk in range(blocks):
            rows, y = proj(hq_scr, blk)
            for hh in range(heads):
                cols = slice(hh * HEAD_DIM, (hh + 1) * HEAD_DIM)
                qh = _rms_scale(y[:, cols]) * nq_ref[...] * q_scale
                kq_ref[rows, cols] = qh.astype(kq_ref.dtype)


def _kvq(x2d, kv_norm, q_in_norm, w_kvq, k_norm, q_norm):
    t, d = x2d.shape
    tm, tn = ROW_TILE, 1024
    n_proj = d // tn
    blocks = tm // MOBA_BLOCK
    kern = functools.partial(_kvq_kernel, tm=tm, tn=tn, n_proj=n_proj, q_scale=HEAD_DIM ** -0.5)

    def kq_map(i, j):
        return i, jnp.where(j < n_proj, j, jnp.maximum(j - n_proj, n_proj - 1))

    kq, vt, km = pl.pallas_call(
        kern,
        grid=(t // tm, 3 * n_proj),
        in_specs=[
            pl.BlockSpec((tm, d), lambda i, j: (i, 0)),
            pl.BlockSpec((1, d), lambda i, j: (0, 0)),
            pl.BlockSpec((1, d), lambda i, j: (0, 0)),
            pl.BlockSpec((d, tn), lambda i, j: (0, j)),
            pl.BlockSpec((1, HEAD_DIM), lambda i, j: (0, 0)),
            pl.BlockSpec((1, HEAD_DIM), lambda i, j: (0, 0)),
        ],
        out_specs=[
            pl.BlockSpec((tm, tn), kq_map),
            pl.BlockSpec((tn, tm), lambda i, j: (jnp.clip(j - n_proj, 0, n_proj - 1), i)),
            pl.BlockSpec((1, blocks, tn), lambda i, j: (i, 0, jnp.minimum(j, n_proj - 1))),
        ],
        out_shape=[jax.ShapeDtypeStruct((t, 2 * d), BF16),
                   jax.ShapeDtypeStruct((d, t), BF16),
                   jax.ShapeDtypeStruct((t // tm, blocks, d), F32)],
        scratch_shapes=[pltpu.VMEM((tm, d), BF16), pltpu.VMEM((tm, d), BF16)],
        compiler_params=_params("parallel", "arbitrary"),
        name="kvq",
    )(x2d, kv_norm.reshape(1, d), q_in_norm.reshape(1, d), w_kvq,
      k_norm.reshape(1, HEAD_DIM), q_norm.reshape(1, HEAD_DIM))
    return kq, vt, km.reshape(t // MOBA_BLOCK, d)


_NT = (((1,), (1,)), ((), ()))


def _attn_kernel(q_ref, k_ref, vt_ref, km_ref, o_ref, s_scr, p_scr, *, n_blocks):
    bs = MOBA_BLOCK
    sub = 8
    grp = bs // sub
    km = km_ref[...].astype(BF16)
    blk_id = lax.broadcasted_iota(jnp.int32, (n_blocks, bs), 0)
    k_pos = lax.broadcasted_iota(jnp.int32, (bs, bs), 0)
    q_pos = lax.broadcasted_iota(jnp.int32, (bs, bs), 1)
    neg_inf = -jnp.inf
    causal = jnp.where(k_pos <= q_pos, 0.0, neg_inf).reshape(grp, sub, bs)

    def scores(i):
        q = q_ref[i * bs:(i + 1) * bs, :]
        nk = (i + 1) * bs
        s_scr[i % 2, 0:nk, :] = lax.dot_general(k_ref[0:nk, :], q, _NT, preferred_element_type=F32)
        if i == 0:
            return None
        return lax.dot_general(km, q, _NT, preferred_element_type=F32)

    gate = scores(0)
    for i in range(n_blocks):
        nk = (i + 1) * bs
        s_ref = s_scr.at[i % 2]
        p_ref = p_scr.at[i % 2]
        next_gate = scores(i + 1) if i + 1 < n_blocks else None

        def piece(jb):
            return s_ref[jb * bs:(jb + 1) * bs, :].reshape(grp, sub, bs)

        biases = []
        if i > 0:
            rank = jnp.zeros((n_blocks, bs), F32)
            for m in range(i):
                gm = jnp.broadcast_to(gate[m:m + 1, :], (n_blocks, bs))
                beats = jnp.logical_or(gm > gate, jnp.logical_and(gm == gate, blk_id > m))
                rank = rank + jnp.where(beats, 1.0, 0.0)
            sel = jnp.logical_and(blk_id < i, rank < MOBA_TOPK)
            bias = jnp.where(sel, 0.0, neg_inf)
            biases = [jnp.broadcast_to(bias[jb:jb + 1, :], (sub, bs)) for jb in range(i)]
        m8 = jnp.max(piece(i) + causal, axis=0)
        for jb in range(i):
            m8 = jnp.maximum(m8, jnp.max(piece(jb), axis=0) + biases[jb])
        m_b = jnp.broadcast_to(jnp.max(m8, axis=0, keepdims=True), (sub, bs))
        l8 = jnp.zeros((sub, bs), F32)
        for jb in range(i + 1):
            if jb < i:
                p = jnp.exp(piece(jb) - (m_b - biases[jb])[None])
            else:
                p = jnp.exp(piece(jb) + causal - m_b[None])
            l8 = l8 + jnp.sum(p, axis=0)
            p_ref[jb * bs:(jb + 1) * bs, :] = p.reshape(bs, bs).astype(BF16)
        denom = jnp.sum(l8, axis=0, keepdims=True)
        o_t = jnp.dot(vt_ref[:, 0:nk], p_ref[0:nk, :], preferred_element_type=F32)
        o_ref[i * bs:(i + 1) * bs, :] = (o_t / denom).T.astype(o_ref.dtype)
        gate = next_gate


def _attention(kq, vt, km, batch, seq, d):
    n_heads = d // HEAD_DIM
    n_blocks = seq // MOBA_BLOCK
    return pl.pallas_call(
        functools.partial(_attn_kernel, n_blocks=n_blocks),
        grid=(batch, n_heads),
        in_specs=[
            pl.BlockSpec((seq, HEAD_DIM), lambda b, h: (b, n_heads + h)),
            pl.BlockSpec((seq, HEAD_DIM), lambda b, h: (b, h)),
            pl.BlockSpec((HEAD_DIM, seq), lambda b, h: (h, b)),
            pl.BlockSpec((n_blocks, HEAD_DIM), lambda b, h: (b, h)),
        ],
        out_specs=pl.BlockSpec((seq, HEAD_DIM), lambda b, h: (b, h)),
        out_shape=jax.ShapeDtypeStruct((batch * seq, d), BF16),
        scratch_shapes=[pltpu.VMEM((2, seq, MOBA_BLOCK), F32), pltpu.VMEM((2, seq, MOBA_BLOCK), BF16)],
        compiler_params=_params("parallel", "arbitrary"),
        name="moba_attn",
    )(kq, kq, vt, km)


def kernel(x, a_norm, a_w_in, a_ln_g, a_ln_b, a_w_s, a_b_s, a_w_out, kv_norm, w_k, w_v, k_norm,
           b_norm, b_w_q, b_q_norm, b_w_o, ffn_norm, ffn_w_gate, ffn_w_up, ffn_w_down):
    batch, seq, d = x.shape
    assert seq % MOBA_BLOCK == 0 and seq // MOBA_BLOCK - 1 >= MOBA_TOPK
    n_a = a_norm.shape[0]
    depth = ffn_norm.shape[0]
    bf = lambda w: w.astype(BF16)

    h = x.reshape(batch * seq, d)
    kv = None
    for layer in range(depth):
        if layer < n_a:
            i = layer
            u, v_ln = _sgu_in(h, a_norm[i], a_w_in, i, a_ln_g[i], a_ln_b[i])
            ug = _sgu_gate(u, v_ln, a_w_s[i], a_b_s[i])
            h = _mm_res(ug, a_w_out, i, h)
        else:
            i = layer - n_a
            assert depth - n_a == 1
            w_kvq = jnp.concatenate([bf(w_k), bf(w_v), bf(b_w_q[i])], axis=1)
            kq, vt, km = _kvq(h, kv_norm, b_norm[i], w_kvq, k_norm, b_q_norm[i])
            attn = _attention(kq, vt, km, batch, seq, d)
            h = _mm_res(attn, b_w_o, i, h)
        h = _ffn(h, ffn_norm[layer], ffn_w_gate, ffn_w_up, ffn_w_down, layer)
    return h.reshape(batch, seq, d)
```

```python
import functools

import jax
import jax.numpy as jnp
from jax import lax
from jax.experimental import pallas as pl
from jax.experimental.pallas import tpu as pltpu

F32 = jnp.float32
BF16 = jnp.bfloat16
EPS = 1e-6

SGU_CHUNK = 128
SGU_GROUPS = 16
HEAD_DIM = 128
MOBA_BLOCK = 256
MOBA_TOPK = 3

VMEM_LIMIT_BYTES = 56 * 1024 * 1024
ROW_TILE = 1024
ROW_BLOCK = 256
NORM_ROWS = 128
FFN_ROW_TILE = 2048
FFN_ROW_BLOCK = 512
LOG2_E = 1.4426950408889634
ATTN_LOOKAHEAD = 2


def _params(*sem):
    return pltpu.CompilerParams(dimension_semantics=sem, vmem_limit_bytes=VMEM_LIMIT_BYTES)


def _rows(n_rows, block, body, unroll=False):
    if unroll:
        for r in range(n_rows // block):
            body(r * block)
        return

    def step(r, carry):
        body(pl.multiple_of(r * block, block))
        return carry
    lax.fori_loop(0, n_rows // block, step, 0)


def _rms_scale(x):
    return x * lax.rsqrt(jnp.mean(x * x, axis=-1, keepdims=True) + EPS)


def _sgu_in_kernel(x_ref, g_ref, w_ref, lng_ref, lnb_ref, u_ref, v_ref, h_scr, v_scr,
                   *, tm, tn, n_u, n_v):
    j = pl.program_id(1)
    last = n_u + n_v - 1
    d = n_v * tn

    def norm(r0):
        for r in range(r0, r0 + ROW_BLOCK, NORM_ROWS):
            x = x_ref[r:r + NORM_ROWS, :]
            h_scr[r:r + NORM_ROWS, :] = (_rms_scale(x) * g_ref[...]).astype(BF16)

    def gelu_mm(r0):
        z = jnp.dot(h_scr[r0:r0 + ROW_BLOCK, :], w_ref[...].astype(BF16),
                    preferred_element_type=F32)
        return jax.nn.gelu(z, approximate=True)

    def layer_norm(r0):
        for r in range(r0, r0 + ROW_BLOCK, NORM_ROWS):
            parts = [v_scr[t, r:r + NORM_ROWS, :] for t in range(n_v)]
            mu = sum(jnp.sum(p, axis=-1, keepdims=True) for p in parts) / d
            cen = [p - mu for p in parts]
            var = sum(jnp.sum(c * c, axis=-1, keepdims=True) for c in cen) / d
            inv = lax.rsqrt(var + EPS)
            for t in range(n_v):
                cols = slice(t * tn, (t + 1) * tn)
                y = cen[t] * inv * lng_ref[:, cols] + lnb_ref[:, cols]
                v_ref[r:r + NORM_ROWS, cols] = y.astype(v_ref.dtype)

    def trips(first_step, u_step, last_step):
        for r0 in range(0, tm, ROW_BLOCK):
            if first_step:
                norm(r0)
            z = gelu_mm(r0)
            if u_step:
                u_ref[r0:r0 + ROW_BLOCK, :] = z.astype(u_ref.dtype)
            elif last_step:
                v_scr[n_v - 1, r0:r0 + ROW_BLOCK, :] = z
                layer_norm(r0)
            else:
                v_scr[j - n_u, r0:r0 + ROW_BLOCK, :] = z

    @pl.when(j == 0)
    def _():
        trips(True, True, False)

    @pl.when(jnp.logical_and(j > 0, j < n_u))
    def _():
        trips(False, True, False)

    @pl.when(jnp.logical_and(j >= n_u, j < last))
    def _():
        trips(False, False, False)

    @pl.when(j == last)
    def _():
        trips(False, False, True)


def _sgu_in(x2d, norm_g, w_in, layer, ln_g, ln_b):
    t, d = x2d.shape
    d_sgu = w_in.shape[2] // 2
    tm, tn = ROW_TILE, 512
    n_u = n_v = d_sgu // tn
    kern = functools.partial(_sgu_in_kernel, tm=tm, tn=tn, n_u=n_u, n_v=n_v)
    return pl.pallas_call(
        kern,
        grid=(t // tm, n_u + n_v),
        in_specs=[
            pl.BlockSpec((tm, d), lambda i, j: (i, 0)),
            pl.BlockSpec((1, d), lambda i, j: (0, 0)),
            pl.BlockSpec((None, d, tn), lambda i, j: (layer, 0, j)),
            pl.BlockSpec((1, d_sgu), lambda i, j: (0, 0)),
            pl.BlockSpec((1, d_sgu), lambda i, j: (0, 0)),
        ],
        out_specs=[
            pl.BlockSpec((tm, tn), lambda i, j: (i, jnp.minimum(j, n_u - 1))),
            pl.BlockSpec((tm, d_sgu), lambda i, j: (i, 0)),
        ],
        out_shape=[jax.ShapeDtypeStruct((t, d_sgu), BF16),
                   jax.ShapeDtypeStruct((t, d_sgu), BF16)],
        scratch_shapes=[pltpu.VMEM((tm, d), BF16), pltpu.VMEM((n_v, tm, tn), F32)],
        compiler_params=_params("parallel", "arbitrary"),
        name="sgu_in",
    )(x2d, norm_g.reshape(1, d), w_in, ln_g.reshape(1, d_sgu), ln_b.reshape(1, d_sgu))


def _sgu_gate_kernel(u_ref, v_ref, ws_ref, bs_ref, o_ref, wm_scr, *, tm):
    c = SGU_CHUNK

    @pl.when(pl.program_id(0) == 0)
    def _():
        t_idx = lax.broadcasted_iota(jnp.int32, (c, c), 0)
        s_idx = lax.broadcasted_iota(jnp.int32, (c, c), 1)
        for g in range(SGU_GROUPS):
            wm_scr[g] = jnp.where(s_idx <= t_idx, ws_ref[g], 0.0).astype(BF16)

    def chunk(r0):
        for g in range(SGU_GROUPS):
            cols = slice(g * c, (g + 1) * c)
            mixed = jnp.dot(wm_scr[g], v_ref[pl.ds(r0, c), cols], preferred_element_type=F32)
            gate = mixed + bs_ref[g]
            o_ref[pl.ds(r0, c), cols] = (u_ref[pl.ds(r0, c), cols].astype(F32) * gate).astype(o_ref.dtype)
    _rows(tm, c, chunk)


def _sgu_gate(u, v_ln, w_s, b_s):
    t, d = u.shape
    tm = 512
    kern = functools.partial(_sgu_gate_kernel, tm=tm)
    return pl.pallas_call(
        kern,
        grid=(t // tm,),
        in_specs=[
            pl.BlockSpec((tm, d), lambda i: (i, 0)),
            pl.BlockSpec((tm, d), lambda i: (i, 0)),
            pl.BlockSpec(w_s.shape, lambda i: (0, 0, 0)),
            pl.BlockSpec((SGU_GROUPS, SGU_CHUNK, 1), lambda i: (0, 0, 0)),
        ],
        out_specs=pl.BlockSpec((tm, d), lambda i: (i, 0)),
        out_shape=jax.ShapeDtypeStruct((t, d), BF16),
        scratch_shapes=[pltpu.VMEM((SGU_GROUPS, SGU_CHUNK, SGU_CHUNK), BF16)],
        compiler_params=_params("arbitrary"),
        name="sgu_gate",
    )(u, v_ln, w_s, b_s.reshape(SGU_GROUPS, SGU_CHUNK, 1))


def _mm_res_kernel(a_ref, w_ref, x_ref, o_ref, *, tm, rb):
    def body(r0):
        acc = jnp.dot(a_ref[pl.ds(r0, rb), :], w_ref[...].astype(BF16),
                      preferred_element_type=F32)
        o_ref[pl.ds(r0, rb), :] = x_ref[pl.ds(r0, rb), :] + acc
    _rows(tm, rb, body, unroll=True)


def _mm_res(a, w_stack, layer, x2d):
    t, k = a.shape
    n = w_stack.shape[2]
    tm, tn = FFN_ROW_TILE, 512
    return pl.pallas_call(
        functools.partial(_mm_res_kernel, tm=tm, rb=FFN_ROW_BLOCK),
        grid=(t // tm, n // tn),
        in_specs=[
            pl.BlockSpec((tm, k), lambda i, j: (i, 0)),
            pl.BlockSpec((None, k, tn), lambda i, j: (layer, 0, j)),
            pl.BlockSpec((tm, tn), lambda i, j: (i, j)),
        ],
        out_specs=pl.BlockSpec((tm, tn), lambda i, j: (i, j)),
        out_shape=jax.ShapeDtypeStruct((t, n), F32),
        compiler_params=_params("parallel", "arbitrary"),
        name="mm_res",
    )(a, w_stack, x2d)


def _ffn_kernel(x_hbm, g_ref, wg_ref, wu_ref, wd_ref, o_hbm, acc, h_scr, in_sem, out_sem,
                *, tm, rb):
    i = pl.program_id(0)
    j = pl.program_id(1)
    n_i = pl.num_programs(0)
    n_j = pl.num_programs(1)
    n_blk = tm // rb

    def in_copy(c):
        return pltpu.make_async_copy(x_hbm.at[pl.ds(i * tm + c * rb, rb), :],
                                     acc.at[pl.ds(c * rb, rb), :], in_sem.at[c])

    def out_copy(tile, c):
        return pltpu.make_async_copy(acc.at[pl.ds(c * rb, rb), :],
                                     o_hbm.at[pl.ds(tile * tm + c * rb, rb), :], out_sem.at[c])

    @pl.when(j == 0)
    def _():
        for c in range(n_blk):
            @pl.when(i > 0)
            def _():
                out_copy(i - 1, c).wait()
            in_copy(c).start()
        for c in range(n_blk):
            in_copy(c).wait()

            def norm(r0):
                rows = pl.ds(pl.multiple_of(c * rb + r0, NORM_ROWS), NORM_ROWS)
                h_scr[rows, :] = (_rms_scale(acc[rows, :]) * g_ref[...]).astype(BF16)
            _rows(rb, NORM_ROWS, norm)

    def trips(write_back):
        for r in range(n_blk):
            rows = pl.ds(r * rb, rb)
            hb = h_scr[rows, :]
            gate = jnp.dot(hb, wg_ref[...].astype(BF16), preferred_element_type=F32)
            up = jnp.dot(hb, wu_ref[...].astype(BF16), preferred_element_type=F32)
            act = (jax.nn.silu(gate) * up).astype(BF16)
            acc[rows, :] += jnp.dot(act, wd_ref[...].astype(BF16), preferred_element_type=F32)
            if write_back:
                out_copy(i, r).start()

    @pl.when(j < n_j - 1)
    def _():
        trips(False)

    @pl.when(j == n_j - 1)
    def _():
        trips(True)

    @pl.when(jnp.logical_and(i == n_i - 1, j == n_j - 1))
    def _():
        for c in range(n_blk):
            out_copy(i, c).wait()


def _ffn(x2d, norm_g, w_gate, w_up, w_down, layer):
    t, d = x2d.shape
    f = w_gate.shape[2]
    tm, tf, rb = FFN_ROW_TILE, 512, FFN_ROW_BLOCK
    return pl.pallas_call(
        functools.partial(_ffn_kernel, tm=tm, rb=rb),
        grid=(t // tm, f // tf),
        in_specs=[
            pl.BlockSpec(memory_space=pl.ANY),
            pl.BlockSpec((1, d), lambda i, j: (0, 0)),
            pl.BlockSpec((None, d, tf), lambda i, j: (layer, 0, j)),
            pl.BlockSpec((None, d, tf), lambda i, j: (layer, 0, j)),
            pl.BlockSpec((None, tf, d), lambda i, j: (layer, j, 0)),
        ],
        out_specs=pl.BlockSpec(memory_space=pl.ANY),
        out_shape=jax.ShapeDtypeStruct((t, d), F32),
        scratch_shapes=[pltpu.VMEM((tm, d), F32), pltpu.VMEM((tm, d), BF16),
                        pltpu.SemaphoreType.DMA((tm // rb,)), pltpu.SemaphoreType.DMA((tm // rb,))],
        compiler_params=_params("arbitrary", "arbitrary"),
        name="ffn",
    )(x2d, norm_g.reshape(1, d), w_gate, w_up, w_down)


def _kvq_kernel(x_ref, gkv_ref, gq_ref, w_ref, nk_ref, nq_ref, kq_ref, vt_ref, km_ref,
                hkv_scr, hq_scr, *, tm, tn, n_proj, q_scale):
    j = pl.program_id(1)
    heads = tn // HEAD_DIM
    blocks = tm // MOBA_BLOCK

    def norm(blk):
        for r in range(blk * MOBA_BLOCK, (blk + 1) * MOBA_BLOCK, NORM_ROWS):
            y = _rms_scale(x_ref[r:r + NORM_ROWS, :])
            hkv_scr[r:r + NORM_ROWS, :] = (y * gkv_ref[...]).astype(BF16)
            hq_scr[r:r + NORM_ROWS, :] = (y * gq_ref[...]).astype(BF16)

    def proj(h_scr, blk):
        rows = slice(blk * MOBA_BLOCK, (blk + 1) * MOBA_BLOCK)
        return rows, jnp.dot(h_scr[rows, :], w_ref[...].astype(BF16), preferred_element_type=F32)

    def k_trips(first_step):
        for blk in range(blocks):
            if first_step:
                norm(blk)
            rows, y = proj(hkv_scr, blk)
            for hh in range(heads):
                cols = slice(hh * HEAD_DIM, (hh + 1) * HEAD_DIM)
                kh = _rms_scale(y[:, cols]) * nk_ref[...]
                kq_ref[rows, cols] = kh.astype(kq_ref.dtype)
                km_ref[0, blk:blk + 1, cols] = jnp.mean(kh, axis=0, keepdims=True)

    @pl.when(j == 0)
    def _():
        k_trips(True)

    @pl.when(jnp.logical_and(j > 0, j < n_proj))
    def _():
        k_trips(False)

    @pl.when(jnp.logical_and(j >= n_proj, j < 2 * n_proj))
    def _():
        for blk in range(blocks):
            rows, y = proj(hkv_scr, blk)
            vt_ref[:, rows] = y.T.astype(vt_ref.dtype)

    @pl.when(j >= 2 * n_proj)
    def _():
        for blk in range(blocks):
            rows, y = proj(hq_scr, blk)
            for hh in range(heads):
                cols = slice(hh * HEAD_DIM, (hh + 1) * HEAD_DIM)
                qh = _rms_scale(y[:, cols]) * nq_ref[...] * q_scale
                kq_ref[rows, cols] = qh.astype(kq_ref.dtype)


def _kvq(x2d, kv_norm, q_in_norm, w_kvq, k_norm, q_norm):
    t, d = x2d.shape
    tm, tn = ROW_TILE, 1024
    n_proj = d // tn
    blocks = tm // MOBA_BLOCK
    kern = functools.partial(_kvq_kernel, tm=tm, tn=tn, n_proj=n_proj, q_scale=HEAD_DIM ** -0.5 * LOG2_E)

    def kq_map(i, j):
        return i, jnp.where(j < n_proj, j, jnp.maximum(j - n_proj, n_proj - 1))

    kq, vt, km = pl.pallas_call(
        kern,
        grid=(t // tm, 3 * n_proj),
        in_specs=[
            pl.BlockSpec((tm, d), lambda i, j: (i, 0)),
            pl.BlockSpec((1, d), lambda i, j: (0, 0)),
            pl.BlockSpec((1, d), lambda i, j: (0, 0)),
            pl.BlockSpec((d, tn), lambda i, j: (0, j)),
            pl.BlockSpec((1, HEAD_DIM), lambda i, j: (0, 0)),
            pl.BlockSpec((1, HEAD_DIM), lambda i, j: (0, 0)),
        ],
        out_specs=[
            pl.BlockSpec((tm, tn), kq_map),
            pl.BlockSpec((tn, tm), lambda i, j: (jnp.clip(j - n_proj, 0, n_proj - 1), i)),
            pl.BlockSpec((1, blocks, tn), lambda i, j: (i, 0, jnp.minimum(j, n_proj - 1))),
        ],
        out_shape=[jax.ShapeDtypeStruct((t, 2 * d), BF16),
                   jax.ShapeDtypeStruct((d, t), BF16),
                   jax.ShapeDtypeStruct((t // tm, blocks, d), F32)],
        scratch_shapes=[pltpu.VMEM((tm, d), BF16), pltpu.VMEM((tm, d), BF16)],
        compiler_params=_params("parallel", "arbitrary"),
        name="kvq",
    )(x2d, kv_norm.reshape(1, d), q_in_norm.reshape(1, d), w_kvq,
      k_norm.reshape(1, HEAD_DIM), q_norm.reshape(1, HEAD_DIM))
    return kq, vt, km.reshape(t // MOBA_BLOCK, d)


_NT = (((1,), (1,)), ((), ()))


def _attn_kernel(q_ref, k_ref, vt_ref, km_ref, o_ref, s_scr, p_scr, *, n_blocks):
    bs = MOBA_BLOCK
    sub = 8
    grp = bs // sub
    km = km_ref[...].astype(BF16)
    blk_id = lax.broadcasted_iota(jnp.int32, (n_blocks, bs), 0)
    k_pos = lax.broadcasted_iota(jnp.int32, (bs, bs), 0)
    q_pos = lax.broadcasted_iota(jnp.int32, (bs, bs), 1)
    neg_inf = -jnp.inf
    causal = jnp.where(k_pos <= q_pos, 0.0, neg_inf).reshape(grp, sub, bs)

    def scores(i):
        q = q_ref[i * bs:(i + 1) * bs, :]
        nk = (i + 1) * bs
        s_scr[i % n_slots, 0:nk, :] = lax.dot_general(k_ref[0:nk, :], q, _NT,
                                                      preferred_element_type=F32)
        if i == 0:
            return None
        return lax.dot_general(km, q, _NT, preferred_element_type=F32)

    n_slots = s_scr.shape[0]
    gates = {i: scores(i) for i in range(min(ATTN_LOOKAHEAD, n_blocks))}
    for i in range(n_blocks):
        nk = (i + 1) * bs
        s_ref = s_scr.at[i % n_slots]
        p_ref = p_scr.at[i % 2]
        if i + ATTN_LOOKAHEAD < n_blocks:
            gates[i + ATTN_LOOKAHEAD] = scores(i + ATTN_LOOKAHEAD)
        gate = gates.pop(i)

        def piece(jb):
            return s_ref[jb * bs:(jb + 1) * bs, :].reshape(grp, sub, bs)

        biases = []
        if i > 0:
            rank = jnp.zeros((n_blocks, bs), F32)
            for m in range(i):
                gm = jnp.broadcast_to(gate[m:m + 1, :], (n_blocks, bs))
                beats = jnp.logical_or(gm > gate, jnp.logical_and(gm == gate, blk_id > m))
                rank = rank + jnp.where(beats, 1.0, 0.0)
            sel = jnp.logical_and(blk_id < i, rank < MOBA_TOPK)
            bias = jnp.where(sel, 0.0, neg_inf)
            biases = [jnp.broadcast_to(bias[jb:jb + 1, :], (sub, bs)) for jb in range(i)]
        m8 = jnp.max(piece(i) + causal, axis=0)
        for jb in range(i):
            m8 = jnp.maximum(m8, jnp.max(piece(jb), axis=0) + biases[jb])
        m_b = jnp.broadcast_to(jnp.max(m8, axis=0, keepdims=True), (sub, bs))
        l8 = jnp.zeros((sub, bs), F32)
        for jb in range(i + 1):
            if jb < i:
                p = jnp.exp2(piece(jb) - (m_b - biases[jb])[None])
            else:
                p = jnp.exp2(piece(jb) + causal - m_b[None])
            l8 = l8 + jnp.sum(p, axis=0)
            p_ref[jb * bs:(jb + 1) * bs, :] = p.reshape(bs, bs).astype(BF16)
        denom = jnp.sum(l8, axis=0, keepdims=True)
        o_t = jnp.dot(vt_ref[:, 0:nk], p_ref[0:nk, :], preferred_element_type=F32)
        o_ref[i * bs:(i + 1) * bs, :] = (o_t / denom).T.astype(o_ref.dtype)


def _attention(kq, vt, km, batch, seq, d):
    n_heads = d // HEAD_DIM
    n_blocks = seq // MOBA_BLOCK
    return pl.pallas_call(
        functools.partial(_attn_kernel, n_blocks=n_blocks),
        grid=(batch, n_heads),
        in_specs=[
            pl.BlockSpec((seq, HEAD_DIM), lambda b, h: (b, n_heads + h)),
            pl.BlockSpec((seq, HEAD_DIM), lambda b, h: (b, h)),
            pl.BlockSpec((HEAD_DIM, seq), lambda b, h: (h, b)),
            pl.BlockSpec((n_blocks, HEAD_DIM), lambda b, h: (b, h)),
        ],
        out_specs=pl.BlockSpec((seq, HEAD_DIM), lambda b, h: (b, h)),
        out_shape=jax.ShapeDtypeStruct((batch * seq, d), BF16),
        scratch_shapes=[pltpu.VMEM((ATTN_LOOKAHEAD + 1, seq, MOBA_BLOCK), F32),
                        pltpu.VMEM((2, seq, MOBA_BLOCK), BF16)],
        compiler_params=_params("parallel", "arbitrary"),
        name="moba_attn",
    )(kq, kq, vt, km)


def kernel(x, a_norm, a_w_in, a_ln_g, a_ln_b, a_w_s, a_b_s, a_w_out, kv_norm, w_k, w_v, k_norm,
           b_norm, b_w_q, b_q_norm, b_w_o, ffn_norm, ffn_w_gate, ffn_w_up, ffn_w_down):
    batch, seq, d = x.shape
    assert seq % MOBA_BLOCK == 0 and seq // MOBA_BLOCK - 1 >= MOBA_TOPK
    n_a = a_norm.shape[0]
    depth = ffn_norm.shape[0]
    bf = lambda w: w.astype(BF16)

    h = x.reshape(batch * seq, d)
    kv = None
    for layer in range(depth):
        if layer < n_a:
            i = layer
            u, v_ln = _sgu_in(h, a_norm[i], a_w_in, i, a_ln_g[i], a_ln_b[i])
            ug = _sgu_gate(u, v_ln, a_w_s[i], a_b_s[i])
            h = _mm_res(ug, a_w_out, i, h)
        else:
            i = layer - n_a
            assert depth - n_a == 1
            w_kvq = jnp.concatenate([bf(w_k), bf(w_v), bf(b_w_q[i])], axis=1)
            kq, vt, km = _kvq(h, kv_norm, b_norm[i], w_kvq, k_norm, b_q_norm[i])
            attn = _attention(kq, vt, km, batch, seq, d)
            h = _mm_res(attn, b_w_o, i, h)
        h = _ffn(h, ffn_norm[layer], ffn_w_gate, ffn_w_up, ffn_w_down, layer)
    return h.reshape(batch, seq, d)
```

```python
import functools

import jax
import jax.numpy as jnp
from jax import lax
from jax.experimental import pallas as pl
from jax.experimental.pallas import tpu as pltpu

F32 = jnp.float32
BF16 = jnp.bfloat16
EPS = 1e-6

SGU_CHUNK = 128
SGU_GROUPS = 16
HEAD_DIM = 128
MOBA_BLOCK = 256
MOBA_TOPK = 3

VMEM_LIMIT_BYTES = 56 * 1024 * 1024
ROW_TILE = 1024
ROW_BLOCK = 256
NORM_ROWS = 128
FFN_ROW_TILE = 2048
FFN_ROW_BLOCK = 512
LOG2_E = 1.4426950408889634
ATTN_LOOKAHEAD = 2


def _params(*sem):
    return pltpu.CompilerParams(dimension_semantics=sem, vmem_limit_bytes=VMEM_LIMIT_BYTES)


def _rows(n_rows, block, body, unroll=False):
    if unroll:
        for r in range(n_rows // block):
            body(r * block)
        return

    def step(r, carry):
        body(pl.multiple_of(r * block, block))
        return carry
    lax.fori_loop(0, n_rows // block, step, 0)


def _rms_scale(x):
    return x * lax.rsqrt(jnp.mean(x * x, axis=-1, keepdims=True) + EPS)


def _sgu_kernel(x_ref, g_ref, w_ref, lng_ref, lnb_ref, ws_ref, bs_ref, o_ref,
                h_scr, u_scr, v_scr, wm_scr, *, tm, tn, n_u, n_v):
    j = pl.program_id(1)
    last = n_u + n_v - 1
    d = n_v * tn
    c = SGU_CHUNK

    def mask_weights():
        t_idx = lax.broadcasted_iota(jnp.int32, (c, c), 0)
        s_idx = lax.broadcasted_iota(jnp.int32, (c, c), 1)
        for g in range(SGU_GROUPS):
            wm_scr[g] = jnp.where(s_idx <= t_idx, ws_ref[g], 0.0).astype(BF16)

    def norm(r0):
        for r in range(r0, r0 + ROW_BLOCK, NORM_ROWS):
            x = x_ref[r:r + NORM_ROWS, :]
            h_scr[r:r + NORM_ROWS, :] = (_rms_scale(x) * g_ref[...]).astype(BF16)

    def gelu_mm(r0):
        z = jnp.dot(h_scr[r0:r0 + ROW_BLOCK, :], w_ref[...].astype(BF16),
                    preferred_element_type=F32)
        return jax.nn.gelu(z, approximate=True)

    def gate_rows(r0):
        for r in range(r0, r0 + ROW_BLOCK, c):
            parts = [v_scr[t, r:r + c, :] for t in range(n_v)]
            mu = sum(jnp.sum(p, axis=-1, keepdims=True) for p in parts) / d
            cen = [p - mu for p in parts]
            var = sum(jnp.sum(x * x, axis=-1, keepdims=True) for x in cen) / d
            inv = lax.rsqrt(var + EPS)
            for t in range(n_v):
                cols = slice(t * tn, (t + 1) * tn)
                v_ln = (cen[t] * inv * lng_ref[:, cols] + lnb_ref[:, cols]).astype(BF16)
                for g in range(t * tn // c, (t + 1) * tn // c):
                    off = g * c - t * tn
                    mixed = jnp.dot(wm_scr[g], v_ln[:, off:off + c], preferred_element_type=F32)
                    gate = mixed + bs_ref[g]
                    u = u_scr[t, r:r + c, off:off + c].astype(F32)
                    o_ref[r:r + c, g * c:(g + 1) * c] = (u * gate).astype(o_ref.dtype)

    def trips(first_step, u_step, last_step):
        for r0 in range(0, tm, ROW_BLOCK):
            if first_step:
                norm(r0)
            z = gelu_mm(r0)
            if u_step:
                u_scr[j, r0:r0 + ROW_BLOCK, :] = z.astype(BF16)
            elif last_step:
                v_scr[n_v - 1, r0:r0 + ROW_BLOCK, :] = z
                gate_rows(r0)
            else:
                v_scr[j - n_u, r0:r0 + ROW_BLOCK, :] = z

    @pl.when(j == 0)
    def _():
        mask_weights()
        trips(True, True, False)

    @pl.when(jnp.logical_and(j > 0, j < n_u))
    def _():
        trips(False, True, False)

    @pl.when(jnp.logical_and(j >= n_u, j < last))
    def _():
        trips(False, False, False)

    @pl.when(j == last)
    def _():
        trips(False, False, True)


def _sgu(x2d, norm_g, w_in, layer, ln_g, ln_b, w_s, b_s):
    t, d = x2d.shape
    d_sgu = w_in.shape[2] // 2
    tm, tn = ROW_TILE, 512
    n_u = n_v = d_sgu // tn
    assert d_sgu == SGU_GROUPS * SGU_CHUNK and tn % SGU_CHUNK == 0
    kern = functools.partial(_sgu_kernel, tm=tm, tn=tn, n_u=n_u, n_v=n_v)
    return pl.pallas_call(
        kern,
        grid=(t // tm, n_u + n_v),
        in_specs=[
            pl.BlockSpec((tm, d), lambda i, j: (i, 0)),
            pl.BlockSpec((1, d), lambda i, j: (0, 0)),
            pl.BlockSpec((None, d, tn), lambda i, j: (layer, 0, j)),
            pl.BlockSpec((1, d_sgu), lambda i, j: (0, 0)),
            pl.BlockSpec((1, d_sgu), lambda i, j: (0, 0)),
            pl.BlockSpec(w_s.shape, lambda i, j: (0, 0, 0)),
            pl.BlockSpec((SGU_GROUPS, SGU_CHUNK, 1), lambda i, j: (0, 0, 0)),
        ],
        out_specs=pl.BlockSpec((tm, d_sgu), lambda i, j: (i, 0)),
        out_shape=jax.ShapeDtypeStruct((t, d_sgu), BF16),
        scratch_shapes=[pltpu.VMEM((tm, d), BF16), pltpu.VMEM((n_u, tm, tn), BF16),
                        pltpu.VMEM((n_v, tm, tn), F32),
                        pltpu.VMEM((SGU_GROUPS, SGU_CHUNK, SGU_CHUNK), BF16)],
        compiler_params=_params("parallel", "arbitrary"),
        name="sgu",
    )(x2d, norm_g.reshape(1, d), w_in, ln_g.reshape(1, d_sgu), ln_b.reshape(1, d_sgu),
      w_s, b_s.reshape(SGU_GROUPS, SGU_CHUNK, 1))


def _mm_res_kernel(a_ref, w_ref, x_ref, o_ref, *, tm, rb):
    def body(r0):
        acc = jnp.dot(a_ref[pl.ds(r0, rb), :], w_ref[...].astype(BF16),
                      preferred_element_type=F32)
        o_ref[pl.ds(r0, rb), :] = x_ref[pl.ds(r0, rb), :] + acc
    _rows(tm, rb, body, unroll=True)


def _mm_res(a, w_stack, layer, x2d):
    t, k = a.shape
    n = w_stack.shape[2]
    tm, tn = FFN_ROW_TILE, 512
    return pl.pallas_call(
        functools.partial(_mm_res_kernel, tm=tm, rb=FFN_ROW_BLOCK),
        grid=(t // tm, n // tn),
        in_specs=[
            pl.BlockSpec((tm, k), lambda i, j: (i, 0)),
            pl.BlockSpec((None, k, tn), lambda i, j: (layer, 0, j)),
            pl.BlockSpec((tm, tn), lambda i, j: (i, j)),
        ],
        out_specs=pl.BlockSpec((tm, tn), lambda i, j: (i, j)),
        out_shape=jax.ShapeDtypeStruct((t, n), F32),
        compiler_params=_params("parallel", "arbitrary"),
        name="mm_res",
    )(a, w_stack, x2d)


def _ffn_kernel(x_hbm, g_ref, wg_ref, wu_ref, wd_ref, o_hbm, acc, h_scr, in_sem, out_sem,
                *, tm, rb):
    i = pl.program_id(0)
    j = pl.program_id(1)
    n_i = pl.num_programs(0)
    n_j = pl.num_programs(1)
    n_blk = tm // rb

    def in_copy(c):
        return pltpu.make_async_copy(x_hbm.at[pl.ds(i * tm + c * rb, rb), :],
                                     acc.at[pl.ds(c * rb, rb), :], in_sem.at[c])

    def out_copy(tile, c):
        return pltpu.make_async_copy(acc.at[pl.ds(c * rb, rb), :],
                                     o_hbm.at[pl.ds(tile * tm + c * rb, rb), :], out_sem.at[c])

    @pl.when(j == 0)
    def _():
        for c in range(n_blk):
            @pl.when(i > 0)
            def _():
                out_copy(i - 1, c).wait()
            in_copy(c).start()
        for c in range(n_blk):
            in_copy(c).wait()

            def norm(r0):
                rows = pl.ds(pl.multiple_of(c * rb + r0, NORM_ROWS), NORM_ROWS)
                h_scr[rows, :] = (_rms_scale(acc[rows, :]) * g_ref[...]).astype(BF16)
            _rows(rb, NORM_ROWS, norm)

    def trips(write_back):
        for r in range(n_blk):
            rows = pl.ds(r * rb, rb)
            hb = h_scr[rows, :]
            gate = jnp.dot(hb, wg_ref[...].astype(BF16), preferred_element_type=F32)
            up = jnp.dot(hb, wu_ref[...].astype(BF16), preferred_element_type=F32)
            act = (jax.nn.silu(gate) * up).astype(BF16)
            acc[rows, :] += jnp.dot(act, wd_ref[...].astype(BF16), preferred_element_type=F32)
            if write_back:
                out_copy(i, r).start()

    @pl.when(j < n_j - 1)
    def _():
        trips(False)

    @pl.when(j == n_j - 1)
    def _():
        trips(True)

    @pl.when(jnp.logical_and(i == n_i - 1, j == n_j - 1))
    def _():
        for c in range(n_blk):
            out_copy(i, c).wait()


def _ffn(x2d, norm_g, w_gate, w_up, w_down, layer):
    t, d = x2d.shape
    f = w_gate.shape[2]
    tm, tf, rb = FFN_ROW_TILE, 512, FFN_ROW_BLOCK
    return pl.pallas_call(
        functools.partial(_ffn_kernel, tm=tm, rb=rb),
        grid=(t // tm, f // tf),
        in_specs=[
            pl.BlockSpec(memory_space=pl.ANY),
            pl.BlockSpec((1, d), lambda i, j: (0, 0)),
            pl.BlockSpec((None, d, tf), lambda i, j: (layer, 0, j)),
            pl.BlockSpec((None, d, tf), lambda i, j: (layer, 0, j)),
            pl.BlockSpec((None, tf, d), lambda i, j: (layer, j, 0)),
        ],
        out_specs=pl.BlockSpec(memory_space=pl.ANY),
        out_shape=jax.ShapeDtypeStruct((t, d), F32),
        scratch_shapes=[pltpu.VMEM((tm, d), F32), pltpu.VMEM((tm, d), BF16),
                        pltpu.SemaphoreType.DMA((tm // rb,)), pltpu.SemaphoreType.DMA((tm // rb,))],
        compiler_params=_params("arbitrary", "arbitrary"),
        name="ffn",
    )(x2d, norm_g.reshape(1, d), w_gate, w_up, w_down)


def _kvq_kernel(x_ref, gkv_ref, gq_ref, w_ref, nk_ref, nq_ref, kq_ref, vt_ref, km_ref,
                hkv_scr, hq_scr, *, tm, tn, n_proj, q_scale):
    j = pl.program_id(1)
    heads = tn // HEAD_DIM
    blocks = tm // MOBA_BLOCK

    def norm(blk):
        for r in range(blk * MOBA_BLOCK, (blk + 1) * MOBA_BLOCK, NORM_ROWS):
            y = _rms_scale(x_ref[r:r + NORM_ROWS, :])
            hkv_scr[r:r + NORM_ROWS, :] = (y * gkv_ref[...]).astype(BF16)
            hq_scr[r:r + NORM_ROWS, :] = (y * gq_ref[...]).astype(BF16)

    def proj(h_scr, blk):
        rows = slice(blk * MOBA_BLOCK, (blk + 1) * MOBA_BLOCK)
        return rows, jnp.dot(h_scr[rows, :], w_ref[...].astype(BF16), preferred_element_type=F32)

    def k_trips(first_step):
        for blk in range(blocks):
            if first_step:
                norm(blk)
            rows, y = proj(hkv_scr, blk)
            for hh in range(heads):
                cols = slice(hh * HEAD_DIM, (hh + 1) * HEAD_DIM)
                kh = _rms_scale(y[:, cols]) * nk_ref[...]
                kq_ref[rows, cols] = kh.astype(kq_ref.dtype)
                km_ref[0, blk:blk + 1, cols] = jnp.mean(kh, axis=0, keepdims=True)

    @pl.when(j == 0)
    def _():
        k_trips(True)

    @pl.when(jnp.logical_and(j > 0, j < n_proj))
    def _():
        k_trips(False)

    @pl.when(jnp.logical_and(j >= n_proj, j < 2 * n_proj))
    def _():
        for blk in range(blocks):
            rows, y = proj(hkv_scr, blk)
            vt_ref[:, rows] = y.T.astype(vt_ref.dtype)

    @pl.when(j >= 2 * n_proj)
    def _():
        for blk in range(blocks):
            rows, y = proj(hq_scr, blk)
            for hh in range(heads):
                cols = slice(hh * HEAD_DIM, (hh + 1) * HEAD_DIM)
                qh = _rms_scale(y[:, cols]) * nq_ref[...] * q_scale
                kq_ref[rows, cols] = qh.astype(kq_ref.dtype)


def _kvq(x2d, kv_norm, q_in_norm, w_kvq, k_norm, q_norm):
    t, d = x2d.shape
    tm, tn = ROW_TILE, 1024
    n_proj = d // tn
    blocks = tm // MOBA_BLOCK
    kern = functools.partial(_kvq_kernel, tm=tm, tn=tn, n_proj=n_proj, q_scale=HEAD_DIM ** -0.5 * LOG2_E)

    def kq_map(i, j):
        return i, jnp.where(j < n_proj, j, jnp.maximum(j - n_proj, n_proj - 1))

    kq, vt, km = pl.pallas_call(
        kern,
        grid=(t // tm, 3 * n_proj),
        in_specs=[
            pl.BlockSpec((tm, d), lambda i, j: (i, 0)),
            pl.BlockSpec((1, d), lambda i, j: (0, 0)),
            pl.BlockSpec((1, d), lambda i, j: (0, 0)),
            pl.BlockSpec((d, tn), lambda i, j: (0, j)),
            pl.BlockSpec((1, HEAD_DIM), lambda i, j: (0, 0)),
            pl.BlockSpec((1, HEAD_DIM), lambda i, j: (0, 0)),
        ],
        out_specs=[
            pl.BlockSpec((tm, tn), kq_map),
            pl.BlockSpec((tn, tm), lambda i, j: (jnp.clip(j - n_proj, 0, n_proj - 1), i)),
            pl.BlockSpec((1, blocks, tn), lambda i, j: (i, 0, jnp.minimum(j, n_proj - 1))),
        ],
        out_shape=[jax.ShapeDtypeStruct((t, 2 * d), BF16),
                   jax.ShapeDtypeStruct((d, t), BF16),
                   jax.ShapeDtypeStruct((t // tm, blocks, d), F32)],
        scratch_shapes=[pltpu.VMEM((tm, d), BF16), pltpu.VMEM((tm, d), BF16)],
        compiler_params=_params("parallel", "arbitrary"),
        name="kvq",
    )(x2d, kv_norm.reshape(1, d), q_in_norm.reshape(1, d), w_kvq,
      k_norm.reshape(1, HEAD_DIM), q_norm.reshape(1, HEAD_DIM))
    return kq, vt, km.reshape(t // MOBA_BLOCK, d)


_NT = (((1,), (1,)), ((), ()))


def _attn_kernel(q_ref, k_ref, vt_ref, km_ref, o_ref, s_scr, p_scr, *, n_blocks):
    bs = MOBA_BLOCK
    sub = 8
    grp = bs // sub
    km = km_ref[...].astype(BF16)
    blk_id = lax.broadcasted_iota(jnp.int32, (n_blocks, bs), 0)
    k_pos = lax.broadcasted_iota(jnp.int32, (bs, bs), 0)
    q_pos = lax.broadcasted_iota(jnp.int32, (bs, bs), 1)
    neg_inf = -jnp.inf
    causal = jnp.where(k_pos <= q_pos, 0.0, neg_inf).reshape(grp, sub, bs)

    n_slots = s_scr.shape[0]

    def scores(pos):
        i = order[pos]
        q = q_ref[i * bs:(i + 1) * bs, :]
        nk = (i + 1) * bs
        s_scr[pos % n_slots, 0:nk, :] = lax.dot_general(k_ref[0:nk, :], q, _NT,
                                                        preferred_element_type=F32)
        if i == 0:
            return None
        return lax.dot_general(km, q, _NT, preferred_element_type=F32)

    order = list(range(n_blocks - 1, -1, -1))
    gates = {pos: scores(pos) for pos in range(min(ATTN_LOOKAHEAD, n_blocks))}
    for pos, i in enumerate(order):
        nk = (i + 1) * bs
        s_ref = s_scr.at[pos % n_slots]
        p_ref = p_scr.at[pos % 2]
        if pos + ATTN_LOOKAHEAD < n_blocks:
            gates[pos + ATTN_LOOKAHEAD] = scores(pos + ATTN_LOOKAHEAD)
        gate = gates.pop(pos)

        def piece(jb):
            return s_ref[jb * bs:(jb + 1) * bs, :].reshape(grp, sub, bs)

        biases = []
        if i > 0:
            rank = jnp.zeros((n_blocks, bs), F32)
            for m in range(i):
                gm = jnp.broadcast_to(gate[m:m + 1, :], (n_blocks, bs))
                beats = jnp.logical_or(gm > gate, jnp.logical_and(gm == gate, blk_id > m))
                rank = rank + jnp.where(beats, 1.0, 0.0)
            sel = jnp.logical_and(blk_id < i, rank < MOBA_TOPK)
            bias = jnp.where(sel, 0.0, neg_inf)
            biases = [jnp.broadcast_to(bias[jb:jb + 1, :], (sub, bs)) for jb in range(i)]
        m8 = jnp.max(piece(i) + causal, axis=0)
        for jb in range(i):
            m8 = jnp.maximum(m8, jnp.max(piece(jb), axis=0) + biases[jb])
        m_b = jnp.broadcast_to(jnp.max(m8, axis=0, keepdims=True), (sub, bs))
        l8 = jnp.zeros((sub, bs), F32)
        for jb in range(i + 1):
            if jb < i:
                p = jnp.exp2(piece(jb) - (m_b - biases[jb])[None])
            else:
                p = jnp.exp2(piece(jb) + causal - m_b[None])
            l8 = l8 + jnp.sum(p, axis=0)
            p_ref[jb * bs:(jb + 1) * bs, :] = p.reshape(bs, bs).astype(BF16)
        denom = jnp.sum(l8, axis=0, keepdims=True)
        o_t = jnp.dot(vt_ref[:, 0:nk], p_ref[0:nk, :], preferred_element_type=F32)
        o_ref[i * bs:(i + 1) * bs, :] = (o_t / denom).T.astype(o_ref.dtype)


def _attention(kq, vt, km, batch, seq, d):
    n_heads = d // HEAD_DIM
    n_blocks = seq // MOBA_BLOCK
    return pl.pallas_call(
        functools.partial(_attn_kernel, n_blocks=n_blocks),
        grid=(batch, n_heads),
        in_specs=[
            pl.BlockSpec((seq, HEAD_DIM), lambda b, h: (b, n_heads + h)),
            pl.BlockSpec((seq, HEAD_DIM), lambda b, h: (b, h)),
            pl.BlockSpec((HEAD_DIM, seq), lambda b, h: (h, b)),
            pl.BlockSpec((n_blocks, HEAD_DIM), lambda b, h: (b, h)),
        ],
        out_specs=pl.BlockSpec((seq, HEAD_DIM), lambda b, h: (b, h)),
        out_shape=jax.ShapeDtypeStruct((batch * seq, d), BF16),
        scratch_shapes=[pltpu.VMEM((ATTN_LOOKAHEAD + 1, seq, MOBA_BLOCK), F32),
                        pltpu.VMEM((2, seq, MOBA_BLOCK), BF16)],
        compiler_params=_params("parallel", "arbitrary"),
        name="moba_attn",
    )(kq, kq, vt, km)


def kernel(x, a_norm, a_w_in, a_ln_g, a_ln_b, a_w_s, a_b_s, a_w_out, kv_norm, w_k, w_v, k_norm,
           b_norm, b_w_q, b_q_norm, b_w_o, ffn_norm, ffn_w_gate, ffn_w_up, ffn_w_down):
    batch, seq, d = x.shape
    assert seq % MOBA_BLOCK == 0 and seq // MOBA_BLOCK - 1 >= MOBA_TOPK
    n_a = a_norm.shape[0]
    depth = ffn_norm.shape[0]
    bf = lambda w: w.astype(BF16)

    h = x.reshape(batch * seq, d)
    kv = None
    for layer in range(depth):
        if layer < n_a:
            i = layer
            ug = _sgu(h, a_norm[i], a_w_in, i, a_ln_g[i], a_ln_b[i], a_w_s[i], a_b_s[i])
            h = _mm_res(ug, a_w_out, i, h)
        else:
            i = layer - n_a
            assert depth - n_a == 1
            w_kvq = jnp.concatenate([bf(w_k), bf(w_v), bf(b_w_q[i])], axis=1)
            kq, vt, km = _kvq(h, kv_norm, b_norm[i], w_kvq, k_norm, b_q_norm[i])
            attn = _attention(kq, vt, km, batch, seq, d)
            h = _mm_res(attn, b_w_o, i, h)
        h = _ffn(h, ffn_norm[layer], ffn_w_gate, ffn_w_up, ffn_w_down, layer)
    return h.reshape(batch, seq, d)
```

```python
import functools

import jax
import jax.numpy as jnp
from jax import lax
from jax.experimental import pallas as pl
from jax.experimental.pallas import tpu as pltpu

F32 = jnp.float32
BF16 = jnp.bfloat16
EPS = 1e-6

SGU_CHUNK = 128
SGU_GROUPS = 16
HEAD_DIM = 128
MOBA_BLOCK = 256
MOBA_TOPK = 3

VMEM_LIMIT_BYTES = 56 * 1024 * 1024
ROW_TILE = 1024
ROW_BLOCK = 256
NORM_ROWS = 128
FFN_ROW_TILE = 2048
FFN_ROW_BLOCK = 512
MM_RES_ROWS = 512
LOG2_E = 1.4426950408889634
ATTN_HEADS_PER_STEP = 2
ATTN_LOOKAHEAD = 3


def _params(*sem):
    return pltpu.CompilerParams(dimension_semantics=sem, vmem_limit_bytes=VMEM_LIMIT_BYTES)


def _rows(n_rows, block, body, unroll=False):
    if unroll:
        for r in range(n_rows // block):
            body(r * block)
        return

    def step(r, carry):
        body(pl.multiple_of(r * block, block))
        return carry
    lax.fori_loop(0, n_rows // block, step, 0)


def _rms_scale(x):
    return x * lax.rsqrt(jnp.mean(x * x, axis=-1, keepdims=True) + EPS)


def _sgu_kernel(x_ref, g_ref, w_ref, lng_ref, lnb_ref, ws_ref, bs_ref, o_ref,
                h_scr, u_scr, v_scr, wm_scr, *, tm, tn, n_u, n_v):
    j = pl.program_id(1)
    last = n_u + n_v - 1
    d = n_v * tn
    c = SGU_CHUNK

    def mask_weights():
        t_idx = lax.broadcasted_iota(jnp.int32, (c, c), 0)
        s_idx = lax.broadcasted_iota(jnp.int32, (c, c), 1)
        for g in range(SGU_GROUPS):
            wm_scr[g] = jnp.where(s_idx <= t_idx, ws_ref[g], 0.0).astype(BF16)

    def norm(r0):
        for r in range(r0, r0 + ROW_BLOCK, NORM_ROWS):
            x = x_ref[r:r + NORM_ROWS, :]
            h_scr[r:r + NORM_ROWS, :] = (_rms_scale(x) * g_ref[...]).astype(BF16)

    def gelu_mm(r0):
        z = jnp.dot(h_scr[r0:r0 + ROW_BLOCK, :], w_ref[...].astype(BF16),
                    preferred_element_type=F32)
        return jax.nn.gelu(z, approximate=True)

    def gate_rows(r0):
        for r in range(r0, r0 + ROW_BLOCK, c):
            parts = [v_scr[t, r:r + c, :] for t in range(n_v)]
            mu = sum(jnp.sum(p, axis=-1, keepdims=True) for p in parts) / d
            cen = [p - mu for p in parts]
            var = sum(jnp.sum(x * x, axis=-1, keepdims=True) for x in cen) / d
            inv = lax.rsqrt(var + EPS)
            for t in range(n_v):
                cols = slice(t * tn, (t + 1) * tn)
                v_ln = (cen[t] * inv * lng_ref[:, cols] + lnb_ref[:, cols]).astype(BF16)
                for g in range(t * tn // c, (t + 1) * tn // c):
                    off = g * c - t * tn
                    mixed = jnp.dot(wm_scr[g], v_ln[:, off:off + c], preferred_element_type=F32)
                    gate = mixed + bs_ref[g]
                    u = u_scr[t, r:r + c, off:off + c].astype(F32)
                    o_ref[r:r + c, g * c:(g + 1) * c] = (u * gate).astype(o_ref.dtype)

    def trips(first_step, u_step, last_step):
        for r0 in range(0, tm, ROW_BLOCK):
            if first_step:
                norm(r0)
            z = gelu_mm(r0)
            if u_step:
                u_scr[j, r0:r0 + ROW_BLOCK, :] = z.astype(BF16)
            elif last_step:
                v_scr[n_v - 1, r0:r0 + ROW_BLOCK, :] = z
                gate_rows(r0)
            else:
                v_scr[j - n_u, r0:r0 + ROW_BLOCK, :] = z

    @pl.when(j == 0)
    def _():
        mask_weights()
        trips(True, True, False)

    @pl.when(jnp.logical_and(j > 0, j < n_u))
    def _():
        trips(False, True, False)

    @pl.when(jnp.logical_and(j >= n_u, j < last))
    def _():
        trips(False, False, False)

    @pl.when(j == last)
    def _():
        trips(False, False, True)


def _sgu(x2d, norm_g, w_in, layer, ln_g, ln_b, w_s, b_s):
    t, d = x2d.shape
    d_sgu = w_in.shape[2] // 2
    tm, tn = ROW_TILE, 512
    n_u = n_v = d_sgu // tn
    assert d_sgu == SGU_GROUPS * SGU_CHUNK and tn % SGU_CHUNK == 0
    kern = functools.partial(_sgu_kernel, tm=tm, tn=tn, n_u=n_u, n_v=n_v)
    return pl.pallas_call(
        kern,
        grid=(t // tm, n_u + n_v),
        in_specs=[
            pl.BlockSpec((tm, d), lambda i, j: (i, 0)),
            pl.BlockSpec((1, d), lambda i, j: (0, 0)),
            pl.BlockSpec((None, d, tn), lambda i, j: (layer, 0, j)),
            pl.BlockSpec((1, d_sgu), lambda i, j: (0, 0)),
            pl.BlockSpec((1, d_sgu), lambda i, j: (0, 0)),
            pl.BlockSpec(w_s.shape, lambda i, j: (0, 0, 0)),
            pl.BlockSpec((SGU_GROUPS, SGU_CHUNK, 1), lambda i, j: (0, 0, 0)),
        ],
        out_specs=pl.BlockSpec((tm, d_sgu), lambda i, j: (i, 0)),
        out_shape=jax.ShapeDtypeStruct((t, d_sgu), BF16),
        scratch_shapes=[pltpu.VMEM((tm, d), BF16), pltpu.VMEM((n_u, tm, tn), BF16),
                        pltpu.VMEM((n_v, tm, tn), F32),
                        pltpu.VMEM((SGU_GROUPS, SGU_CHUNK, SGU_CHUNK), BF16)],
        compiler_params=_params("parallel", "arbitrary"),
        name="sgu",
    )(x2d, norm_g.reshape(1, d), w_in, ln_g.reshape(1, d_sgu), ln_b.reshape(1, d_sgu),
      w_s, b_s.reshape(SGU_GROUPS, SGU_CHUNK, 1))


def _mm_res_kernel(a_ref, w_ref, x_ref, o_ref, *, tn):
    for c0 in range(0, o_ref.shape[1], tn):
        cols = slice(c0, c0 + tn)
        acc = jnp.dot(a_ref[...], w_ref[:, cols].astype(BF16), preferred_element_type=F32)
        o_ref[:, cols] = x_ref[:, cols] + acc


def _mm_res(a, w_stack, layer, x2d):
    t, k = a.shape
    n = w_stack.shape[2]
    tm = MM_RES_ROWS
    return pl.pallas_call(
        functools.partial(_mm_res_kernel, tn=512),
        grid=(t // tm,),
        in_specs=[
            pl.BlockSpec((tm, k), lambda i: (i, 0)),
            pl.BlockSpec((None, k, n), lambda i: (layer, 0, 0), pipeline_mode=pl.Buffered(1)),
            pl.BlockSpec((tm, n), lambda i: (i, 0)),
        ],
        out_specs=pl.BlockSpec((tm, n), lambda i: (i, 0)),
        out_shape=jax.ShapeDtypeStruct((t, n), F32),
        compiler_params=_params("parallel"),
        name="mm_res",
    )(a, w_stack, x2d)


def _ffn_kernel(x_hbm, g_ref, wg_ref, wu_ref, wd_ref, o_hbm, acc, h_scr, in_sem, out_sem,
                *, tm, rb):
    i = pl.program_id(0)
    j = pl.program_id(1)
    n_i = pl.num_programs(0)
    n_j = pl.num_programs(1)
    n_blk = tm // rb

    def in_copy(c):
        return pltpu.make_async_copy(x_hbm.at[pl.ds(i * tm + c * rb, rb), :],
                                     acc.at[pl.ds(c * rb, rb), :], in_sem.at[c])

    def out_copy(tile, c):
        return pltpu.make_async_copy(acc.at[pl.ds(c * rb, rb), :],
                                     o_hbm.at[pl.ds(tile * tm + c * rb, rb), :], out_sem.at[c])

    @pl.when(j == 0)
    def _():
        for c in range(n_blk):
            @pl.when(i > 0)
            def _():
                out_copy(i - 1, c).wait()
            in_copy(c).start()
        for c in range(n_blk):
            in_copy(c).wait()

            def norm(r0):
                rows = pl.ds(pl.multiple_of(c * rb + r0, NORM_ROWS), NORM_ROWS)
                h_scr[rows, :] = (_rms_scale(acc[rows, :]) * g_ref[...]).astype(BF16)
            _rows(rb, NORM_ROWS, norm)

    def trips(write_back):
        for r in range(n_blk):
            rows = pl.ds(r * rb, rb)
            hb = h_scr[rows, :]
            gate = jnp.dot(hb, wg_ref[...].astype(BF16), preferred_element_type=F32)
            up = jnp.dot(hb, wu_ref[...].astype(BF16), preferred_element_type=F32)
            act = (jax.nn.silu(gate) * up).astype(BF16)
            acc[rows, :] += jnp.dot(act, wd_ref[...].astype(BF16), preferred_element_type=F32)
            if write_back:
                out_copy(i, r).start()

    @pl.when(j < n_j - 1)
    def _():
        trips(False)

    @pl.when(j == n_j - 1)
    def _():
        trips(True)

    @pl.when(jnp.logical_and(i == n_i - 1, j == n_j - 1))
    def _():
        for c in range(n_blk):
            out_copy(i, c).wait()


def _ffn(x2d, norm_g, w_gate, w_up, w_down, layer):
    t, d = x2d.shape
    f = w_gate.shape[2]
    tm, tf, rb = FFN_ROW_TILE, 512, FFN_ROW_BLOCK
    return pl.pallas_call(
        functools.partial(_ffn_kernel, tm=tm, rb=rb),
        grid=(t // tm, f // tf),
        in_specs=[
            pl.BlockSpec(memory_space=pl.ANY),
            pl.BlockSpec((1, d), lambda i, j: (0, 0)),
            pl.BlockSpec((None, d, tf), lambda i, j: (layer, 0, j)),
            pl.BlockSpec((None, d, tf), lambda i, j: (layer, 0, j)),
            pl.BlockSpec((None, tf, d), lambda i, j: (layer, j, 0)),
        ],
        out_specs=pl.BlockSpec(memory_space=pl.ANY),
        out_shape=jax.ShapeDtypeStruct((t, d), F32),
        scratch_shapes=[pltpu.VMEM((tm, d), F32), pltpu.VMEM((tm, d), BF16),
                        pltpu.SemaphoreType.DMA((tm // rb,)), pltpu.SemaphoreType.DMA((tm // rb,))],
        compiler_params=_params("arbitrary", "arbitrary"),
        name="ffn",
    )(x2d, norm_g.reshape(1, d), w_gate, w_up, w_down)


def _kvq_kernel(x_ref, gkv_ref, gq_ref, w_ref, nk_ref, nq_ref, kq_ref, vt_ref, km_ref,
                hkv_scr, hq_scr, *, tm, tn, n_proj, q_scale):
    j = pl.program_id(1)
    heads = tn // HEAD_DIM
    blocks = tm // MOBA_BLOCK

    def norm(blk):
        for r in range(blk * MOBA_BLOCK, (blk + 1) * MOBA_BLOCK, NORM_ROWS):
            y = _rms_scale(x_ref[r:r + NORM_ROWS, :])
            hkv_scr[r:r + NORM_ROWS, :] = (y * gkv_ref[...]).astype(BF16)
            hq_scr[r:r + NORM_ROWS, :] = (y * gq_ref[...]).astype(BF16)

    def proj(h_scr, blk):
        rows = slice(blk * MOBA_BLOCK, (blk + 1) * MOBA_BLOCK)
        return rows, jnp.dot(h_scr[rows, :], w_ref[...].astype(BF16), preferred_element_type=F32)

    def k_trips(first_step):
        for blk in range(blocks):
            if first_step:
                norm(blk)
            rows, y = proj(hkv_scr, blk)
            for hh in range(heads):
                cols = slice(hh * HEAD_DIM, (hh + 1) * HEAD_DIM)
                kh = _rms_scale(y[:, cols]) * nk_ref[...]
                kq_ref[rows, cols] = kh.astype(kq_ref.dtype)
                km_ref[0, blk:blk + 1, cols] = jnp.mean(kh, axis=0, keepdims=True)

    @pl.when(j == 0)
    def _():
        k_trips(True)

    @pl.when(jnp.logical_and(j > 0, j < n_proj))
    def _():
        k_trips(False)

    @pl.when(jnp.logical_and(j >= n_proj, j < 2 * n_proj))
    def _():
        for blk in range(blocks):
            rows, y = proj(hkv_scr, blk)
            vt_ref[:, rows] = y.T.astype(vt_ref.dtype)

    @pl.when(j >= 2 * n_proj)
    def _():
        for blk in range(blocks):
            rows, y = proj(hq_scr, blk)
            for hh in range(heads):
                cols = slice(hh * HEAD_DIM, (hh + 1) * HEAD_DIM)
                qh = _rms_scale(y[:, cols]) * nq_ref[...] * q_scale
                kq_ref[rows, cols] = qh.astype(kq_ref.dtype)


def _kvq(x2d, kv_norm, q_in_norm, w_kvq, k_norm, q_norm):
    t, d = x2d.shape
    tm, tn = ROW_TILE, 1024
    n_proj = d // tn
    blocks = tm // MOBA_BLOCK
    kern = functools.partial(_kvq_kernel, tm=tm, tn=tn, n_proj=n_proj, q_scale=HEAD_DIM ** -0.5 * LOG2_E)

    def kq_map(i, j):
        return i, jnp.where(j < n_proj, j, jnp.maximum(j - n_proj, n_proj - 1))

    kq, vt, km = pl.pallas_call(
        kern,
        grid=(t // tm, 3 * n_proj),
        in_specs=[
            pl.BlockSpec((tm, d), lambda i, j: (i, 0)),
            pl.BlockSpec((1, d), lambda i, j: (0, 0)),
            pl.BlockSpec((1, d), lambda i, j: (0, 0)),
            pl.BlockSpec((d, tn), lambda i, j: (0, j)),
            pl.BlockSpec((1, HEAD_DIM), lambda i, j: (0, 0)),
            pl.BlockSpec((1, HEAD_DIM), lambda i, j: (0, 0)),
        ],
        out_specs=[
            pl.BlockSpec((tm, tn), kq_map),
            pl.BlockSpec((tn, tm), lambda i, j: (jnp.clip(j - n_proj, 0, n_proj - 1), i)),
            pl.BlockSpec((1, blocks, tn), lambda i, j: (i, 0, jnp.minimum(j, n_proj - 1))),
        ],
        out_shape=[jax.ShapeDtypeStruct((t, 2 * d), BF16),
                   jax.ShapeDtypeStruct((d, t), BF16),
                   jax.ShapeDtypeStruct((t // tm, blocks, d), F32)],
        scratch_shapes=[pltpu.VMEM((tm, d), BF16), pltpu.VMEM((tm, d), BF16)],
        compiler_params=_params("parallel", "arbitrary"),
        name="kvq",
    )(x2d, kv_norm.reshape(1, d), q_in_norm.reshape(1, d), w_kvq,
      k_norm.reshape(1, HEAD_DIM), q_norm.reshape(1, HEAD_DIM))
    return kq, vt, km.reshape(t // MOBA_BLOCK, d)


_NT = (((1,), (1,)), ((), ()))


def _attn_kernel(q_ref, k_ref, vt_ref, km_ref, o_ref, s_scr, p_scr, *, n_blocks):
    bs = MOBA_BLOCK
    sub = 8
    grp = bs // sub
    dh = HEAD_DIM
    n_heads = q_ref.shape[1] // dh
    blk_id = lax.broadcasted_iota(jnp.int32, (n_blocks, bs), 0)
    k_pos = lax.broadcasted_iota(jnp.int32, (bs, bs), 0)
    q_pos = lax.broadcasted_iota(jnp.int32, (bs, bs), 1)
    neg_inf = -jnp.inf
    causal = jnp.where(k_pos <= q_pos, 0.0, neg_inf).reshape(grp, sub, bs)

    n_slots = s_scr.shape[0]

    def scores(job):
        hd, i = jobs[job]
        cols = slice(hd * dh, (hd + 1) * dh)
        q = q_ref[i * bs:(i + 1) * bs, cols]
        nk = (i + 1) * bs
        s_scr[job % n_slots, 0:nk, :] = lax.dot_general(k_ref[0:nk, cols], q, _NT,
                                                        preferred_element_type=F32)
        if i == 0:
            return None
        km = km_ref[:, cols].astype(BF16)
        return lax.dot_general(km, q, _NT, preferred_element_type=F32)

    jobs = [(hd, i) for i in range(n_blocks - 1, -1, -1) for hd in range(n_heads)]
    gates = {job: scores(job) for job in range(min(ATTN_LOOKAHEAD, len(jobs)))}
    for job, (hd, i) in enumerate(jobs):
        cols = slice(hd * dh, (hd + 1) * dh)
        nk = (i + 1) * bs
        s_ref = s_scr.at[job % n_slots]
        p_ref = p_scr.at[job % 2]
        if job + ATTN_LOOKAHEAD < len(jobs):
            gates[job + ATTN_LOOKAHEAD] = scores(job + ATTN_LOOKAHEAD)
        gate = gates.pop(job)

        def piece(jb):
            return s_ref[jb * bs:(jb + 1) * bs, :].reshape(grp, sub, bs)

        biases = []
        if i > 0:
            rank = jnp.zeros((n_blocks, bs), F32)
            for m in range(i):
                gm = jnp.broadcast_to(gate[m:m + 1, :], (n_blocks, bs))
                beats = jnp.logical_or(gm > gate, jnp.logical_and(gm == gate, blk_id > m))
                rank = rank + jnp.where(beats, 1.0, 0.0)
            sel = jnp.logical_and(blk_id < i, rank < MOBA_TOPK)
            bias = jnp.where(sel, 0.0, neg_inf)
            biases = [jnp.broadcast_to(bias[jb:jb + 1, :], (sub, bs)) for jb in range(i)]
        m8 = jnp.max(piece(i) + causal, axis=0)
        for jb in range(i):
            m8 = jnp.maximum(m8, jnp.max(piece(jb), axis=0) + biases[jb])
        m_b = jnp.broadcast_to(jnp.max(m8, axis=0, keepdims=True), (sub, bs))
        l8 = jnp.zeros((sub, bs), F32)
        for jb in range(i + 1):
            if jb < i:
                p = jnp.exp2(piece(jb) - (m_b - biases[jb])[None])
            else:
                p = jnp.exp2(piece(jb) + causal - m_b[None])
            l8 = l8 + jnp.sum(p, axis=0)
            p_ref[jb * bs:(jb + 1) * bs, :] = p.reshape(bs, bs).astype(BF16)
        denom = jnp.sum(l8, axis=0, keepdims=True)
        o_t = jnp.dot(vt_ref[cols, 0:nk], p_ref[0:nk, :], preferred_element_type=F32)
        o_ref[i * bs:(i + 1) * bs, cols] = (o_t / denom).T.astype(o_ref.dtype)


def _attention(kq, vt, km, batch, seq, d):
    n_blocks = seq // MOBA_BLOCK
    w = ATTN_HEADS_PER_STEP * HEAD_DIM
    n_groups = d // w
    return pl.pallas_call(
        functools.partial(_attn_kernel, n_blocks=n_blocks),
        grid=(batch, n_groups),
        in_specs=[
            pl.BlockSpec((seq, w), lambda b, h: (b, n_groups + h)),
            pl.BlockSpec((seq, w), lambda b, h: (b, h)),
            pl.BlockSpec((w, seq), lambda b, h: (h, b)),
            pl.BlockSpec((n_blocks, w), lambda b, h: (b, h)),
        ],
        out_specs=pl.BlockSpec((seq, w), lambda b, h: (b, h)),
        out_shape=jax.ShapeDtypeStruct((batch * seq, d), BF16),
        scratch_shapes=[pltpu.VMEM((ATTN_LOOKAHEAD + 1, seq, MOBA_BLOCK), F32),
                        pltpu.VMEM((2, seq, MOBA_BLOCK), BF16)],
        compiler_params=_params("parallel", "arbitrary"),
        name="moba_attn",
    )(kq, kq, vt, km)


def kernel(x, a_norm, a_w_in, a_ln_g, a_ln_b, a_w_s, a_b_s, a_w_out, kv_norm, w_k, w_v, k_norm,
           b_norm, b_w_q, b_q_norm, b_w_o, ffn_norm, ffn_w_gate, ffn_w_up, ffn_w_down):
    batch, seq, d = x.shape
    assert seq % MOBA_BLOCK == 0 and seq // MOBA_BLOCK - 1 >= MOBA_TOPK
    n_a = a_norm.shape[0]
    depth = ffn_norm.shape[0]
    bf = lambda w: w.astype(BF16)

    h = x.reshape(batch * seq, d)
    kv = None
    for layer in range(depth):
        if layer < n_a:
            i = layer
            ug = _sgu(h, a_norm[i], a_w_in, i, a_ln_g[i], a_ln_b[i], a_w_s[i], a_b_s[i])
            h = _mm_res(ug, a_w_out, i, h)
        else:
            i = layer - n_a
            assert depth - n_a == 1
            w_kvq = jnp.concatenate([bf(w_k), bf(w_v), bf(b_w_q[i])], axis=1)
            kq, vt, km = _kvq(h, kv_norm, b_norm[i], w_kvq, k_norm, b_q_norm[i])
            attn = _attention(kq, vt, km, batch, seq, d)
            h = _mm_res(attn, b_w_o, i, h)
        h = _ffn(h, ffn_norm[layer], ffn_w_gate, ffn_w_up, ffn_w_down, layer)
    return h.reshape(batch, seq, d)
```

```python
import functools

import jax
import jax.numpy as jnp
from jax import lax
from jax.experimental import pallas as pl
from jax.experimental.pallas import tpu as pltpu

F32 = jnp.float32
BF16 = jnp.bfloat16
EPS = 1e-6

SGU_CHUNK = 128
SGU_GROUPS = 16
HEAD_DIM = 128
MOBA_BLOCK = 256
MOBA_TOPK = 3

VMEM_LIMIT_BYTES = 56 * 1024 * 1024
ROW_TILE = 1024
ROW_BLOCK = 256
NORM_ROWS = 128
FFN_ROW_TILE = 2048
FFN_ROW_BLOCK = 512
SGU_ROWS = 256
MM_RES_ROWS = 512
LOG2_E = 1.4426950408889634
ATTN_HEADS_PER_STEP = 2
ATTN_LOOKAHEAD = 3


def _params(*sem):
    return pltpu.CompilerParams(dimension_semantics=sem, vmem_limit_bytes=VMEM_LIMIT_BYTES)


def _rows(n_rows, block, body, unroll=False):
    if unroll:
        for r in range(n_rows // block):
            body(r * block)
        return

    def step(r, carry):
        body(pl.multiple_of(r * block, block))
        return carry
    lax.fori_loop(0, n_rows // block, step, 0)


def _rms_scale(x):
    return x * lax.rsqrt(jnp.mean(x * x, axis=-1, keepdims=True) + EPS)


def _sgu_kernel(x_ref, g_ref, w_ref, lng_ref, lnb_ref, ws_ref, bs_ref, o_ref,
                h_scr, u_scr, v_scr, wm_scr, *, tm, tn, n_u, n_v):
    d = n_v * tn
    c = SGU_CHUNK

    def mask_weights():
        t_idx = lax.broadcasted_iota(jnp.int32, (c, c), 0)
        s_idx = lax.broadcasted_iota(jnp.int32, (c, c), 1)
        for g in range(SGU_GROUPS):
            wm_scr[g] = jnp.where(s_idx <= t_idx, ws_ref[g], 0.0).astype(BF16)

    def gelu_mm(col_tile):
        cols = slice(col_tile * tn, (col_tile + 1) * tn)
        z = jnp.dot(h_scr[...], w_ref[:, cols].astype(BF16), preferred_element_type=F32)
        return jax.nn.gelu(z, approximate=True)

    def gate_rows(slot):
        for r in range(0, tm, c):
            parts = [v_scr[slot, t, r:r + c, :] for t in range(n_v)]
            mu = sum(jnp.sum(p, axis=-1, keepdims=True) for p in parts) / d
            cen = [p - mu for p in parts]
            var = sum(jnp.sum(x * x, axis=-1, keepdims=True) for x in cen) / d
            inv = lax.rsqrt(var + EPS)
            for t in range(n_v):
                cols = slice(t * tn, (t + 1) * tn)
                v_ln = (cen[t] * inv * lng_ref[:, cols] + lnb_ref[:, cols]).astype(BF16)
                for g in range(t * tn // c, (t + 1) * tn // c):
                    off = g * c - t * tn
                    mixed = jnp.dot(wm_scr[g], v_ln[:, off:off + c], preferred_element_type=F32)
                    gate = mixed + bs_ref[g]
                    u = u_scr[slot, t, r:r + c, off:off + c].astype(F32)
                    o_ref[r:r + c, g * c:(g + 1) * c] = (u * gate).astype(o_ref.dtype)

    def project(slot):
        for r in range(0, tm, NORM_ROWS):
            x = x_ref[r:r + NORM_ROWS, :]
            h_scr[r:r + NORM_ROWS, :] = (_rms_scale(x) * g_ref[...]).astype(BF16)
        for t in range(n_u):
            u_scr[slot, t] = gelu_mm(t).astype(BF16)
        for t in range(n_v):
            v_scr[slot, t] = gelu_mm(n_u + t)

    i = pl.program_id(0)
    n_tiles = pl.num_programs(0) - 1

    @pl.when(i == 0)
    def _():
        mask_weights()
        project(0)

    for slot in range(2):
        @pl.when(jnp.logical_and(jnp.logical_and(i > 0, i < n_tiles), i % 2 == slot))
        def _(slot=slot):
            project(slot)
            gate_rows(1 - slot)

        @pl.when(jnp.logical_and(i == n_tiles, i % 2 == slot))
        def _(slot=slot):
            gate_rows(1 - slot)


def _sgu(x2d, norm_g, w_in, layer, ln_g, ln_b, w_s, b_s):
    t, d = x2d.shape
    d_sgu = w_in.shape[2] // 2
    tm, tn = SGU_ROWS, 512
    n_u = n_v = d_sgu // tn
    assert d_sgu == SGU_GROUPS * SGU_CHUNK and tn % SGU_CHUNK == 0
    n_tiles = t // tm
    kern = functools.partial(_sgu_kernel, tm=tm, tn=tn, n_u=n_u, n_v=n_v)
    const = lambda i: (0, 0)
    return pl.pallas_call(
        kern,
        grid=(n_tiles + 1,),
        in_specs=[
            pl.BlockSpec((tm, d), lambda i: (jnp.minimum(i, n_tiles - 1), 0)),
            pl.BlockSpec((1, d), const),
            pl.BlockSpec((None, d, 2 * d_sgu), lambda i: (layer, 0, 0),
                         pipeline_mode=pl.Buffered(1)),
            pl.BlockSpec((1, d_sgu), const),
            pl.BlockSpec((1, d_sgu), const),
            pl.BlockSpec(w_s.shape, lambda i: (0, 0, 0), pipeline_mode=pl.Buffered(1)),
            pl.BlockSpec((SGU_GROUPS, SGU_CHUNK, 1), lambda i: (0, 0, 0)),
        ],
        out_specs=pl.BlockSpec((tm, d_sgu), lambda i: (jnp.maximum(i - 1, 0), 0)),
        out_shape=jax.ShapeDtypeStruct((t, d_sgu), BF16),
        scratch_shapes=[pltpu.VMEM((tm, d), BF16), pltpu.VMEM((2, n_u, tm, tn), BF16),
                        pltpu.VMEM((2, n_v, tm, tn), F32),
                        pltpu.VMEM((SGU_GROUPS, SGU_CHUNK, SGU_CHUNK), BF16)],
        compiler_params=_params("arbitrary"),
        name="sgu",
    )(x2d, norm_g.reshape(1, d), w_in, ln_g.reshape(1, d_sgu), ln_b.reshape(1, d_sgu),
      w_s, b_s.reshape(SGU_GROUPS, SGU_CHUNK, 1))


def _mm_res_kernel(a_ref, w_ref, x_ref, o_ref, *, tn):
    for c0 in range(0, o_ref.shape[1], tn):
        cols = slice(c0, c0 + tn)
        acc = jnp.dot(a_ref[...], w_ref[:, cols].astype(BF16), preferred_element_type=F32)
        o_ref[:, cols] = x_ref[:, cols] + acc


def _mm_res(a, w_stack, layer, x2d):
    t, k = a.shape
    n = w_stack.shape[2]
    tm = MM_RES_ROWS
    return pl.pallas_call(
        functools.partial(_mm_res_kernel, tn=512),
        grid=(t // tm,),
        in_specs=[
            pl.BlockSpec((tm, k), lambda i: (i, 0)),
            pl.BlockSpec((None, k, n), lambda i: (layer, 0, 0), pipeline_mode=pl.Buffered(1)),
            pl.BlockSpec((tm, n), lambda i: (i, 0)),
        ],
        out_specs=pl.BlockSpec((tm, n), lambda i: (i, 0)),
        out_shape=jax.ShapeDtypeStruct((t, n), F32),
        compiler_params=_params("parallel"),
        name="mm_res",
    )(a, w_stack, x2d)


def _ffn_kernel(x_hbm, g_ref, wg_ref, wu_ref, wd_ref, o_hbm, acc, h_scr, in_sem, out_sem,
                *, tm, rb):
    i = pl.program_id(0)
    j = pl.program_id(1)
    n_i = pl.num_programs(0)
    n_j = pl.num_programs(1)
    n_blk = tm // rb

    def in_copy(c):
        return pltpu.make_async_copy(x_hbm.at[pl.ds(i * tm + c * rb, rb), :],
                                     acc.at[pl.ds(c * rb, rb), :], in_sem.at[c])

    def out_copy(tile, c):
        return pltpu.make_async_copy(acc.at[pl.ds(c * rb, rb), :],
                                     o_hbm.at[pl.ds(tile * tm + c * rb, rb), :], out_sem.at[c])

    @pl.when(j == 0)
    def _():
        for c in range(n_blk):
            @pl.when(i > 0)
            def _():
                out_copy(i - 1, c).wait()
            in_copy(c).start()
        for c in range(n_blk):
            in_copy(c).wait()

            def norm(r0):
                rows = pl.ds(pl.multiple_of(c * rb + r0, NORM_ROWS), NORM_ROWS)
                h_scr[rows, :] = (_rms_scale(acc[rows, :]) * g_ref[...]).astype(BF16)
            _rows(rb, NORM_ROWS, norm)

    def trips(write_back):
        for r in range(n_blk):
            rows = pl.ds(r * rb, rb)
            hb = h_scr[rows, :]
            gate = jnp.dot(hb, wg_ref[...].astype(BF16), preferred_element_type=F32)
            up = jnp.dot(hb, wu_ref[...].astype(BF16), preferred_element_type=F32)
            act = (jax.nn.silu(gate) * up).astype(BF16)
            acc[rows, :] += jnp.dot(act, wd_ref[...].astype(BF16), preferred_element_type=F32)
            if write_back:
                out_copy(i, r).start()

    @pl.when(j < n_j - 1)
    def _():
        trips(False)

    @pl.when(j == n_j - 1)
    def _():
        trips(True)

    @pl.when(jnp.logical_and(i == n_i - 1, j == n_j - 1))
    def _():
        for c in range(n_blk):
            out_copy(i, c).wait()


def _ffn(x2d, norm_g, w_gate, w_up, w_down, layer):
    t, d = x2d.shape
    f = w_gate.shape[2]
    tm, tf, rb = FFN_ROW_TILE, 512, FFN_ROW_BLOCK
    return pl.pallas_call(
        functools.partial(_ffn_kernel, tm=tm, rb=rb),
        grid=(t // tm, f // tf),
        in_specs=[
            pl.BlockSpec(memory_space=pl.ANY),
            pl.BlockSpec((1, d), lambda i, j: (0, 0)),
            pl.BlockSpec((None, d, tf), lambda i, j: (layer, 0, j)),
            pl.BlockSpec((None, d, tf), lambda i, j: (layer, 0, j)),
            pl.BlockSpec((None, tf, d), lambda i, j: (layer, j, 0)),
        ],
        out_specs=pl.BlockSpec(memory_space=pl.ANY),
        out_shape=jax.ShapeDtypeStruct((t, d), F32),
        scratch_shapes=[pltpu.VMEM((tm, d), F32), pltpu.VMEM((tm, d), BF16),
                        pltpu.SemaphoreType.DMA((tm // rb,)), pltpu.SemaphoreType.DMA((tm // rb,))],
        compiler_params=_params("arbitrary", "arbitrary"),
        name="ffn",
    )(x2d, norm_g.reshape(1, d), w_gate, w_up, w_down)


def _kvq_kernel(x_ref, gkv_ref, gq_ref, w_ref, nk_ref, nq_ref, kq_ref, vt_ref, km_ref,
                hkv_scr, hq_scr, *, tm, tn, n_proj, q_scale):
    j = pl.program_id(1)
    heads = tn // HEAD_DIM
    blocks = tm // MOBA_BLOCK

    def norm(blk):
        for r in range(blk * MOBA_BLOCK, (blk + 1) * MOBA_BLOCK, NORM_ROWS):
            y = _rms_scale(x_ref[r:r + NORM_ROWS, :])
            hkv_scr[r:r + NORM_ROWS, :] = (y * gkv_ref[...]).astype(BF16)
            hq_scr[r:r + NORM_ROWS, :] = (y * gq_ref[...]).astype(BF16)

    def proj(h_scr, blk):
        rows = slice(blk * MOBA_BLOCK, (blk + 1) * MOBA_BLOCK)
        return rows, jnp.dot(h_scr[rows, :], w_ref[...].astype(BF16), preferred_element_type=F32)

    def k_trips(first_step):
        for blk in range(blocks):
            if first_step:
                norm(blk)
            rows, y = proj(hkv_scr, blk)
            for hh in range(heads):
                cols = slice(hh * HEAD_DIM, (hh + 1) * HEAD_DIM)
                kh = _rms_scale(y[:, cols]) * nk_ref[...]
                kq_ref[rows, cols] = kh.astype(kq_ref.dtype)
                km_ref[0, blk:blk + 1, cols] = jnp.mean(kh, axis=0, keepdims=True)

    @pl.when(j == 0)
    def _():
        k_trips(True)

    @pl.when(jnp.logical_and(j > 0, j < n_proj))
    def _():
        k_trips(False)

    @pl.when(jnp.logical_and(j >= n_proj, j < 2 * n_proj))
    def _():
        for blk in range(blocks):
            rows, y = proj(hkv_scr, blk)
            vt_ref[:, rows] = y.T.astype(vt_ref.dtype)

    @pl.when(j >= 2 * n_proj)
    def _():
        for blk in range(blocks):
            rows, y = proj(hq_scr, blk)
            for hh in range(heads):
                cols = slice(hh * HEAD_DIM, (hh + 1) * HEAD_DIM)
                qh = _rms_scale(y[:, cols]) * nq_ref[...] * q_scale
                kq_ref[rows, cols] = qh.astype(kq_ref.dtype)


def _kvq(x2d, kv_norm, q_in_norm, w_kvq, k_norm, q_norm):
    t, d = x2d.shape
    tm, tn = ROW_TILE, 1024
    n_proj = d // tn
    blocks = tm // MOBA_BLOCK
    kern = functools.partial(_kvq_kernel, tm=tm, tn=tn, n_proj=n_proj, q_scale=HEAD_DIM ** -0.5 * LOG2_E)

    def kq_map(i, j):
        return i, jnp.where(j < n_proj, j, jnp.maximum(j - n_proj, n_proj - 1))

    kq, vt, km = pl.pallas_call(
        kern,
        grid=(t // tm, 3 * n_proj),
        in_specs=[
            pl.BlockSpec((tm, d), lambda i, j: (i, 0)),
            pl.BlockSpec((1, d), lambda i, j: (0, 0)),
            pl.BlockSpec((1, d), lambda i, j: (0, 0)),
            pl.BlockSpec((d, tn), lambda i, j: (0, j)),
            pl.BlockSpec((1, HEAD_DIM), lambda i, j: (0, 0)),
            pl.BlockSpec((1, HEAD_DIM), lambda i, j: (0, 0)),
        ],
        out_specs=[
            pl.BlockSpec((tm, tn), kq_map),
            pl.BlockSpec((tn, tm), lambda i, j: (jnp.clip(j - n_proj, 0, n_proj - 1), i)),
            pl.BlockSpec((1, blocks, tn), lambda i, j: (i, 0, jnp.minimum(j, n_proj - 1))),
        ],
        out_shape=[jax.ShapeDtypeStruct((t, 2 * d), BF16),
                   jax.ShapeDtypeStruct((d, t), BF16),
                   jax.ShapeDtypeStruct((t // tm, blocks, d), F32)],
        scratch_shapes=[pltpu.VMEM((tm, d), BF16), pltpu.VMEM((tm, d), BF16)],
        compiler_params=_params("parallel", "arbitrary"),
        name="kvq",
    )(x2d, kv_norm.reshape(1, d), q_in_norm.reshape(1, d), w_kvq,
      k_norm.reshape(1, HEAD_DIM), q_norm.reshape(1, HEAD_DIM))
    return kq, vt, km.reshape(t // MOBA_BLOCK, d)


_NT = (((1,), (1,)), ((), ()))


def _attn_kernel(q_ref, k_ref, vt_ref, km_ref, o_ref, s_scr, p_scr, *, n_blocks):
    bs = MOBA_BLOCK
    sub = 8
    grp = bs // sub
    dh = HEAD_DIM
    n_heads = q_ref.shape[1] // dh
    blk_id = lax.broadcasted_iota(jnp.int32, (n_blocks, bs), 0)
    k_pos = lax.broadcasted_iota(jnp.int32, (bs, bs), 0)
    q_pos = lax.broadcasted_iota(jnp.int32, (bs, bs), 1)
    neg_inf = -jnp.inf
    causal = jnp.where(k_pos <= q_pos, 0.0, neg_inf).reshape(grp, sub, bs)

    n_slots = s_scr.shape[0]

    def scores(job):
        hd, i = jobs[job]
        cols = slice(hd * dh, (hd + 1) * dh)
        q = q_ref[i * bs:(i + 1) * bs, cols]
        nk = (i + 1) * bs
        s_scr[job % n_slots, 0:nk, :] = lax.dot_general(k_ref[0:nk, cols], q, _NT,
                                                        preferred_element_type=F32)
        if i == 0:
            return None
        km = km_ref[:, cols].astype(BF16)
        return lax.dot_general(km, q, _NT, preferred_element_type=F32)

    jobs = [(hd, i) for i in range(n_blocks - 1, -1, -1) for hd in range(n_heads)]
    gates = {job: scores(job) for job in range(min(ATTN_LOOKAHEAD, len(jobs)))}
    for job, (hd, i) in enumerate(jobs):
        cols = slice(hd * dh, (hd + 1) * dh)
        nk = (i + 1) * bs
        s_ref = s_scr.at[job % n_slots]
        p_ref = p_scr.at[job % 2]
        if job + ATTN_LOOKAHEAD < len(jobs):
            gates[job + ATTN_LOOKAHEAD] = scores(job + ATTN_LOOKAHEAD)
        gate = gates.pop(job)

        def piece(jb):
            return s_ref[jb * bs:(jb + 1) * bs, :].reshape(grp, sub, bs)

        biases = []
        if i > 0:
            rank = jnp.zeros((n_blocks, bs), F32)
            for m in range(i):
                gm = jnp.broadcast_to(gate[m:m + 1, :], (n_blocks, bs))
                beats = jnp.logical_or(gm > gate, jnp.logical_and(gm == gate, blk_id > m))
                rank = rank + jnp.where(beats, 1.0, 0.0)
            sel = jnp.logical_and(blk_id < i, rank < MOBA_TOPK)
            bias = jnp.where(sel, 0.0, neg_inf)
            biases = [jnp.broadcast_to(bias[jb:jb + 1, :], (sub, bs)) for jb in range(i)]
        m8 = jnp.max(piece(i) + causal, axis=0)
        for jb in range(i):
            m8 = jnp.maximum(m8, jnp.max(piece(jb), axis=0) + biases[jb])
        m_b = jnp.broadcast_to(jnp.max(m8, axis=0, keepdims=True), (sub, bs))
        l8 = jnp.zeros((sub, bs), F32)
        for jb in range(i + 1):
            if jb < i:
                p = jnp.exp2(piece(jb) - (m_b - biases[jb])[None])
            else:
                p = jnp.exp2(piece(jb) + causal - m_b[None])
            l8 = l8 + jnp.sum(p, axis=0)
            p_ref[jb * bs:(jb + 1) * bs, :] = p.reshape(bs, bs).astype(BF16)
        denom = jnp.sum(l8, axis=0, keepdims=True)
        o_t = jnp.dot(vt_ref[cols, 0:nk], p_ref[0:nk, :], preferred_element_type=F32)
        o_ref[i * bs:(i + 1) * bs, cols] = (o_t / denom).T.astype(o_ref.dtype)


def _attention(kq, vt, km, batch, seq, d):
    n_blocks = seq // MOBA_BLOCK
    w = ATTN_HEADS_PER_STEP * HEAD_DIM
    n_groups = d // w
    return pl.pallas_call(
        functools.partial(_attn_kernel, n_blocks=n_blocks),
        grid=(batch, n_groups),
        in_specs=[
            pl.BlockSpec((seq, w), lambda b, h: (b, n_groups + h)),
            pl.BlockSpec((seq, w), lambda b, h: (b, h)),
            pl.BlockSpec((w, seq), lambda b, h: (h, b)),
            pl.BlockSpec((n_blocks, w), lambda b, h: (b, h)),
        ],
        out_specs=pl.BlockSpec((seq, w), lambda b, h: (b, h)),
        out_shape=jax.ShapeDtypeStruct((batch * seq, d), BF16),
        scratch_shapes=[pltpu.VMEM((ATTN_LOOKAHEAD + 1, seq, MOBA_BLOCK), F32),
                        pltpu.VMEM((2, seq, MOBA_BLOCK), BF16)],
        compiler_params=_params("parallel", "arbitrary"),
        name="moba_attn",
    )(kq, kq, vt, km)


def kernel(x, a_norm, a_w_in, a_ln_g, a_ln_b, a_w_s, a_b_s, a_w_out, kv_norm, w_k, w_v, k_norm,
           b_norm, b_w_q, b_q_norm, b_w_o, ffn_norm, ffn_w_gate, ffn_w_up, ffn_w_down):
    batch, seq, d = x.shape
    assert seq % MOBA_BLOCK == 0 and seq // MOBA_BLOCK - 1 >= MOBA_TOPK
    n_a = a_norm.shape[0]
    depth = ffn_norm.shape[0]
    bf = lambda w: w.astype(BF16)

    h = x.reshape(batch * seq, d)
    kv = None
    for layer in range(depth):
        if layer < n_a:
            i = layer
            ug = _sgu(h, a_norm[i], a_w_in, i, a_ln_g[i], a_ln_b[i], a_w_s[i], a_b_s[i])
            h = _mm_res(ug, a_w_out, i, h)
        else:
            i = layer - n_a
            assert depth - n_a == 1
            w_kvq = jnp.concatenate([bf(w_k), bf(w_v), bf(b_w_q[i])], axis=1)
            kq, vt, km = _kvq(h, kv_norm, b_norm[i], w_kvq, k_norm, b_q_norm[i])
            attn = _attention(kq, vt, km, batch, seq, d)
            h = _mm_res(attn, b_w_o, i, h)
        h = _ffn(h, ffn_norm[layer], ffn_w_gate, ffn_w_up, ffn_w_down, layer)
    return h.reshape(batch, seq, d)
```

```python
import functools

import jax
import jax.numpy as jnp
from jax import lax
from jax.experimental import pallas as pl
from jax.experimental.pallas import tpu as pltpu

F32 = jnp.float32
BF16 = jnp.bfloat16
EPS = 1e-6

SGU_CHUNK = 128
SGU_GROUPS = 16
HEAD_DIM = 128
MOBA_BLOCK = 256
MOBA_TOPK = 3

VMEM_LIMIT_BYTES = 60 * 1024 * 1024
ROW_TILE = 1024
ROW_BLOCK = 256
NORM_ROWS = 128
FFN_ROW_TILE = 2048
FFN_ROW_BLOCK = 512
SGU_ROWS = 512
WEIGHT_STAGE_COLS = 256
MM_RES_ROWS = 512
LOG2_E = 1.4426950408889634
ATTN_HEADS_PER_STEP = 2
ATTN_LOOKAHEAD = 3


def _params(*sem):
    return pltpu.CompilerParams(dimension_semantics=sem, vmem_limit_bytes=VMEM_LIMIT_BYTES)


def _rows(n_rows, block, body, unroll=False):
    if unroll:
        for r in range(n_rows // block):
            body(r * block)
        return

    def step(r, carry):
        body(pl.multiple_of(r * block, block))
        return carry
    lax.fori_loop(0, n_rows // block, step, 0)


def _rms_scale(x):
    return x * lax.rsqrt(jnp.mean(x * x, axis=-1, keepdims=True) + EPS)


def _fetch_weight_bf16(w_hbm, layer, w_bf, stage, sem):
    k, n = w_bf.shape
    cw = stage.shape[2]
    n_chunks = n // cw

    def copy(ch):
        return pltpu.make_async_copy(w_hbm.at[layer, :, pl.ds(ch * cw, cw)], stage.at[ch % 2],
                                     sem.at[ch % 2])

    copy(0).start()
    for ch in range(n_chunks):
        if ch + 1 < n_chunks:
            copy(ch + 1).start()
        copy(ch).wait()
        for r in range(0, k, ROW_BLOCK):
            w_bf[r:r + ROW_BLOCK, ch * cw:(ch + 1) * cw] = (
                stage[ch % 2, r:r + ROW_BLOCK, :].astype(BF16))


def _sgu_kernel(x_ref, g_ref, w_hbm, lng_ref, lnb_ref, ws_ref, bs_ref, o_ref,
                w_bf, stage, sem, h_scr, u_scr, v_scr, wm_scr, *, layer, tm, tn, n_u, n_v):
    d = n_v * tn
    c = SGU_CHUNK

    def mask_weights():
        t_idx = lax.broadcasted_iota(jnp.int32, (c, c), 0)
        s_idx = lax.broadcasted_iota(jnp.int32, (c, c), 1)
        for g in range(SGU_GROUPS):
            wm_scr[g] = jnp.where(s_idx <= t_idx, ws_ref[g], 0.0).astype(BF16)

    def gelu_mm(col_tile):
        cols = slice(col_tile * tn, (col_tile + 1) * tn)
        z = jnp.dot(h_scr[...], w_bf[:, cols], preferred_element_type=F32)
        return jax.nn.gelu(z, approximate=True)

    def gate_rows(slot):
        for r in range(0, tm, c):
            parts = [v_scr[slot, t, r:r + c, :] for t in range(n_v)]
            mu = sum(jnp.sum(p, axis=-1, keepdims=True) for p in parts) / d
            cen = [p - mu for p in parts]
            var = sum(jnp.sum(x * x, axis=-1, keepdims=True) for x in cen) / d
            inv = lax.rsqrt(var + EPS)
            for t in range(n_v):
                cols = slice(t * tn, (t + 1) * tn)
                v_ln = (cen[t] * inv * lng_ref[:, cols] + lnb_ref[:, cols]).astype(BF16)
                for g in range(t * tn // c, (t + 1) * tn // c):
                    off = g * c - t * tn
                    mixed = jnp.dot(wm_scr[g], v_ln[:, off:off + c], preferred_element_type=F32)
                    gate = mixed + bs_ref[g]
                    u = u_scr[slot, t, r:r + c, off:off + c].astype(F32)
                    o_ref[r:r + c, g * c:(g + 1) * c] = (u * gate).astype(o_ref.dtype)

    def project(slot):
        for r in range(0, tm, NORM_ROWS):
            x = x_ref[r:r + NORM_ROWS, :]
            h_scr[r:r + NORM_ROWS, :] = (_rms_scale(x) * g_ref[...]).astype(BF16)
        for t in range(n_u):
            u_scr[slot, t] = gelu_mm(t).astype(BF16)
        for t in range(n_v):
            v_scr[slot, t] = gelu_mm(n_u + t)

    i = pl.program_id(0)
    n_tiles = pl.num_programs(0) - 1

    @pl.when(i == 0)
    def _():
        _fetch_weight_bf16(w_hbm, layer, w_bf, stage, sem)
        mask_weights()
        project(0)

    for slot in range(2):
        @pl.when(jnp.logical_and(jnp.logical_and(i > 0, i < n_tiles), i % 2 == slot))
        def _(slot=slot):
            project(slot)
            gate_rows(1 - slot)

        @pl.when(jnp.logical_and(i == n_tiles, i % 2 == slot))
        def _(slot=slot):
            gate_rows(1 - slot)


def _sgu(x2d, norm_g, w_in, layer, ln_g, ln_b, w_s, b_s):
    t, d = x2d.shape
    d_sgu = w_in.shape[2] // 2
    tm, tn = SGU_ROWS, 512
    n_u = n_v = d_sgu // tn
    assert d_sgu == SGU_GROUPS * SGU_CHUNK and tn % SGU_CHUNK == 0
    n_tiles = t // tm
    kern = functools.partial(_sgu_kernel, layer=layer, tm=tm, tn=tn, n_u=n_u, n_v=n_v)
    const = lambda i: (0, 0)
    return pl.pallas_call(
        kern,
        grid=(n_tiles + 1,),
        in_specs=[
            pl.BlockSpec((tm, d), lambda i: (jnp.minimum(i, n_tiles - 1), 0)),
            pl.BlockSpec((1, d), const),
            pl.BlockSpec(memory_space=pl.ANY),
            pl.BlockSpec((1, d_sgu), const),
            pl.BlockSpec((1, d_sgu), const),
            pl.BlockSpec(w_s.shape, lambda i: (0, 0, 0), pipeline_mode=pl.Buffered(1)),
            pl.BlockSpec((SGU_GROUPS, SGU_CHUNK, 1), lambda i: (0, 0, 0)),
        ],
        out_specs=pl.BlockSpec((tm, d_sgu), lambda i: (jnp.maximum(i - 1, 0), 0)),
        out_shape=jax.ShapeDtypeStruct((t, d_sgu), BF16),
        scratch_shapes=[pltpu.VMEM((d, 2 * d_sgu), BF16),
                        pltpu.VMEM((2, d, WEIGHT_STAGE_COLS), F32), pltpu.SemaphoreType.DMA((2,)),
                        pltpu.VMEM((tm, d), BF16), pltpu.VMEM((2, n_u, tm, tn), BF16),
                        pltpu.VMEM((2, n_v, tm, tn), F32),
                        pltpu.VMEM((SGU_GROUPS, SGU_CHUNK, SGU_CHUNK), BF16)],
        compiler_params=_params("arbitrary"),
        name="sgu",
    )(x2d, norm_g.reshape(1, d), w_in, ln_g.reshape(1, d_sgu), ln_b.reshape(1, d_sgu),
      w_s, b_s.reshape(SGU_GROUPS, SGU_CHUNK, 1))


def _mm_res_kernel(a_ref, w_hbm, x_ref, o_ref, w_bf, stage, sem, *, layer, tn):
    @pl.when(pl.program_id(0) == 0)
    def _():
        _fetch_weight_bf16(w_hbm, layer, w_bf, stage, sem)

    for c0 in range(0, o_ref.shape[1], tn):
        cols = slice(c0, c0 + tn)
        acc = jnp.dot(a_ref[...], w_bf[:, cols], preferred_element_type=F32)
        o_ref[:, cols] = x_ref[:, cols] + acc


def _mm_res(a, w_stack, layer, x2d):
    t, k = a.shape
    n = w_stack.shape[2]
    tm = MM_RES_ROWS
    return pl.pallas_call(
        functools.partial(_mm_res_kernel, layer=layer, tn=512),
        grid=(t // tm,),
        in_specs=[
            pl.BlockSpec((tm, k), lambda i: (i, 0)),
            pl.BlockSpec(memory_space=pl.ANY),
            pl.BlockSpec((tm, n), lambda i: (i, 0)),
        ],
        out_specs=pl.BlockSpec((tm, n), lambda i: (i, 0)),
        out_shape=jax.ShapeDtypeStruct((t, n), F32),
        scratch_shapes=[pltpu.VMEM((k, n), BF16), pltpu.VMEM((2, k, WEIGHT_STAGE_COLS), F32),
                        pltpu.SemaphoreType.DMA((2,))],
        compiler_params=_params("arbitrary"),
        name="mm_res",
    )(a, w_stack, x2d)


def _ffn_kernel(x_hbm, g_ref, wg_ref, wu_ref, wd_ref, o_hbm, acc, h_scr, in_sem, out_sem,
                *, tm, rb):
    i = pl.program_id(0)
    j = pl.program_id(1)
    n_i = pl.num_programs(0)
    n_j = pl.num_programs(1)
    n_blk = tm // rb

    def in_copy(c):
        return pltpu.make_async_copy(x_hbm.at[pl.ds(i * tm + c * rb, rb), :],
                                     acc.at[pl.ds(c * rb, rb), :], in_sem.at[c])

    def out_copy(tile, c):
        return pltpu.make_async_copy(acc.at[pl.ds(c * rb, rb), :],
                                     o_hbm.at[pl.ds(tile * tm + c * rb, rb), :], out_sem.at[c])

    @pl.when(j == 0)
    def _():
        for c in range(n_blk):
            @pl.when(i > 0)
            def _():
                out_copy(i - 1, c).wait()
            in_copy(c).start()
        for c in range(n_blk):
            in_copy(c).wait()

            def norm(r0):
                rows = pl.ds(pl.multiple_of(c * rb + r0, NORM_ROWS), NORM_ROWS)
                h_scr[rows, :] = (_rms_scale(acc[rows, :]) * g_ref[...]).astype(BF16)
            _rows(rb, NORM_ROWS, norm)

    def trips(write_back):
        for r in range(n_blk):
            rows = pl.ds(r * rb, rb)
            hb = h_scr[rows, :]
            gate = jnp.dot(hb, wg_ref[...].astype(BF16), preferred_element_type=F32)
            up = jnp.dot(hb, wu_ref[...].astype(BF16), preferred_element_type=F32)
            act = (jax.nn.silu(gate) * up).astype(BF16)
            acc[rows, :] += jnp.dot(act, wd_ref[...].astype(BF16), preferred_element_type=F32)
            if write_back:
                out_copy(i, r).start()

    @pl.when(j < n_j - 1)
    def _():
        trips(False)

    @pl.when(j == n_j - 1)
    def _():
        trips(True)

    @pl.when(jnp.logical_and(i == n_i - 1, j == n_j - 1))
    def _():
        for c in range(n_blk):
            out_copy(i, c).wait()


def _ffn(x2d, norm_g, w_gate, w_up, w_down, layer):
    t, d = x2d.shape
    f = w_gate.shape[2]
    tm, tf, rb = FFN_ROW_TILE, 512, FFN_ROW_BLOCK
    return pl.pallas_call(
        functools.partial(_ffn_kernel, tm=tm, rb=rb),
        grid=(t // tm, f // tf),
        in_specs=[
            pl.BlockSpec(memory_space=pl.ANY),
            pl.BlockSpec((1, d), lambda i, j: (0, 0)),
            pl.BlockSpec((None, d, tf), lambda i, j: (layer, 0, j)),
            pl.BlockSpec((None, d, tf), lambda i, j: (layer, 0, j)),
            pl.BlockSpec((None, tf, d), lambda i, j: (layer, j, 0)),
        ],
        out_specs=pl.BlockSpec(memory_space=pl.ANY),
        out_shape=jax.ShapeDtypeStruct((t, d), F32),
        scratch_shapes=[pltpu.VMEM((tm, d), F32), pltpu.VMEM((tm, d), BF16),
                        pltpu.SemaphoreType.DMA((tm // rb,)), pltpu.SemaphoreType.DMA((tm // rb,))],
        compiler_params=_params("arbitrary", "arbitrary"),
        name="ffn",
    )(x2d, norm_g.reshape(1, d), w_gate, w_up, w_down)


def _kvq_kernel(x_ref, gkv_ref, gq_ref, w_ref, nk_ref, nq_ref, kq_ref, vt_ref, km_ref,
                hkv_scr, hq_scr, *, tm, tn, n_proj, q_scale):
    j = pl.program_id(1)
    heads = tn // HEAD_DIM
    blocks = tm // MOBA_BLOCK

    def norm(blk):
        for r in range(blk * MOBA_BLOCK, (blk + 1) * MOBA_BLOCK, NORM_ROWS):
            y = _rms_scale(x_ref[r:r + NORM_ROWS, :])
            hkv_scr[r:r + NORM_ROWS, :] = (y * gkv_ref[...]).astype(BF16)
            hq_scr[r:r + NORM_ROWS, :] = (y * gq_ref[...]).astype(BF16)

    def proj(h_scr, blk):
        rows = slice(blk * MOBA_BLOCK, (blk + 1) * MOBA_BLOCK)
        return rows, jnp.dot(h_scr[rows, :], w_ref[...].astype(BF16), preferred_element_type=F32)

    def k_trips(first_step):
        for blk in range(blocks):
            if first_step:
                norm(blk)
            rows, y = proj(hkv_scr, blk)
            for hh in range(heads):
                cols = slice(hh * HEAD_DIM, (hh + 1) * HEAD_DIM)
                kh = _rms_scale(y[:, cols]) * nk_ref[...]
                kq_ref[rows, cols] = kh.astype(kq_ref.dtype)
                km_ref[0, blk:blk + 1, cols] = jnp.mean(kh, axis=0, keepdims=True)

    @pl.when(j == 0)
    def _():
        k_trips(True)

    @pl.when(jnp.logical_and(j > 0, j < n_proj))
    def _():
        k_trips(False)

    @pl.when(jnp.logical_and(j >= n_proj, j < 2 * n_proj))
    def _():
        for blk in range(blocks):
            rows, y = proj(hkv_scr, blk)
            vt_ref[:, rows] = y.T.astype(vt_ref.dtype)

    @pl.when(j >= 2 * n_proj)
    def _():
        for blk in range(blocks):
            rows, y = proj(hq_scr, blk)
            for hh in range(heads):
                cols = slice(hh * HEAD_DIM, (hh + 1) * HEAD_DIM)
                qh = _rms_scale(y[:, cols]) * nq_ref[...] * q_scale
                kq_ref[rows, cols] = qh.astype(kq_ref.dtype)


def _kvq(x2d, kv_norm, q_in_norm, w_kvq, k_norm, q_norm):
    t, d = x2d.shape
    tm, tn = ROW_TILE, 1024
    n_proj = d // tn
    blocks = tm // MOBA_BLOCK
    kern = functools.partial(_kvq_kernel, tm=tm, tn=tn, n_proj=n_proj, q_scale=HEAD_DIM ** -0.5 * LOG2_E)

    def kq_map(i, j):
        return i, jnp.where(j < n_proj, j, jnp.maximum(j - n_proj, n_proj - 1))

    kq, vt, km = pl.pallas_call(
        kern,
        grid=(t // tm, 3 * n_proj),
        in_specs=[
            pl.BlockSpec((tm, d), lambda i, j: (i, 0)),
            pl.BlockSpec((1, d), lambda i, j: (0, 0)),
            pl.BlockSpec((1, d), lambda i, j: (0, 0)),
            pl.BlockSpec((d, tn), lambda i, j: (0, j)),
            pl.BlockSpec((1, HEAD_DIM), lambda i, j: (0, 0)),
            pl.BlockSpec((1, HEAD_DIM), lambda i, j: (0, 0)),
        ],
        out_specs=[
            pl.BlockSpec((tm, tn), kq_map),
            pl.BlockSpec((tn, tm), lambda i, j: (jnp.clip(j - n_proj, 0, n_proj - 1), i)),
            pl.BlockSpec((1, blocks, tn), lambda i, j: (i, 0, jnp.minimum(j, n_proj - 1))),
        ],
        out_shape=[jax.ShapeDtypeStruct((t, 2 * d), BF16),
                   jax.ShapeDtypeStruct((d, t), BF16),
                   jax.ShapeDtypeStruct((t // tm, blocks, d), F32)],
        scratch_shapes=[pltpu.VMEM((tm, d), BF16), pltpu.VMEM((tm, d), BF16)],
        compiler_params=_params("parallel", "arbitrary"),
        name="kvq",
    )(x2d, kv_norm.reshape(1, d), q_in_norm.reshape(1, d), w_kvq,
      k_norm.reshape(1, HEAD_DIM), q_norm.reshape(1, HEAD_DIM))
    return kq, vt, km.reshape(t // MOBA_BLOCK, d)


_NT = (((1,), (1,)), ((), ()))


def _attn_kernel(q_ref, k_ref, vt_ref, km_ref, o_ref, s_scr, p_scr, *, n_blocks):
    bs = MOBA_BLOCK
    sub = 8
    grp = bs // sub
    dh = HEAD_DIM
    n_heads = q_ref.shape[1] // dh
    blk_id = lax.broadcasted_iota(jnp.int32, (n_blocks, bs), 0)
    k_pos = lax.broadcasted_iota(jnp.int32, (bs, bs), 0)
    q_pos = lax.broadcasted_iota(jnp.int32, (bs, bs), 1)
    neg_inf = -jnp.inf
    causal = jnp.where(k_pos <= q_pos, 0.0, neg_inf).reshape(grp, sub, bs)

    n_slots = s_scr.shape[0]

    def scores(job):
        hd, i = jobs[job]
        cols = slice(hd * dh, (hd + 1) * dh)
        q = q_ref[i * bs:(i + 1) * bs, cols]
        nk = (i + 1) * bs
        s_scr[job % n_slots, 0:nk, :] = lax.dot_general(k_ref[0:nk, cols], q, _NT,
                                                        preferred_element_type=F32)
        if i == 0:
            return None
        km = km_ref[:, cols].astype(BF16)
        return lax.dot_general(km, q, _NT, preferred_element_type=F32)

    jobs = [(hd, i) for i in range(n_blocks - 1, -1, -1) for hd in range(n_heads)]
    gates = {job: scores(job) for job in range(min(ATTN_LOOKAHEAD, len(jobs)))}
    for job, (hd, i) in enumerate(jobs):
        cols = slice(hd * dh, (hd + 1) * dh)
        nk = (i + 1) * bs
        s_ref = s_scr.at[job % n_slots]
        p_ref = p_scr.at[job % 2]
        if job + ATTN_LOOKAHEAD < len(jobs):
            gates[job + ATTN_LOOKAHEAD] = scores(job + ATTN_LOOKAHEAD)
        gate = gates.pop(job)

        def piece(jb):
            return s_ref[jb * bs:(jb + 1) * bs, :].reshape(grp, sub, bs)

        biases = []
        if i > 0:
            rank = jnp.zeros((n_blocks, bs), F32)
            for m in range(i):
                gm = jnp.broadcast_to(gate[m:m + 1, :], (n_blocks, bs))
                beats = jnp.logical_or(gm > gate, jnp.logical_and(gm == gate, blk_id > m))
                rank = rank + jnp.where(beats, 1.0, 0.0)
            sel = jnp.logical_and(blk_id < i, rank < MOBA_TOPK)
            bias = jnp.where(sel, 0.0, neg_inf)
            biases = [jnp.broadcast_to(bias[jb:jb + 1, :], (sub, bs)) for jb in range(i)]
        m8 = jnp.max(piece(i) + causal, axis=0)
        for jb in range(i):
            m8 = jnp.maximum(m8, jnp.max(piece(jb), axis=0) + biases[jb])
        m_b = jnp.broadcast_to(jnp.max(m8, axis=0, keepdims=True), (sub, bs))
        l8 = jnp.zeros((sub, bs), F32)
        for jb in range(i + 1):
            if jb < i:
                p = jnp.exp2(piece(jb) - (m_b - biases[jb])[None])
            else:
                p = jnp.exp2(piece(jb) + causal - m_b[None])
            l8 = l8 + jnp.sum(p, axis=0)
            p_ref[jb * bs:(jb + 1) * bs, :] = p.reshape(bs, bs).astype(BF16)
        denom = jnp.sum(l8, axis=0, keepdims=True)
        o_t = jnp.dot(vt_ref[cols, 0:nk], p_ref[0:nk, :], preferred_element_type=F32)
        o_ref[i * bs:(i + 1) * bs, cols] = (o_t / denom).T.astype(o_ref.dtype)


def _attention(kq, vt, km, batch, seq, d):
    n_blocks = seq // MOBA_BLOCK
    w = ATTN_HEADS_PER_STEP * HEAD_DIM
    n_groups = d // w
    return pl.pallas_call(
        functools.partial(_attn_kernel, n_blocks=n_blocks),
        grid=(batch, n_groups),
        in_specs=[
            pl.BlockSpec((seq, w), lambda b, h: (b, n_groups + h)),
            pl.BlockSpec((seq, w), lambda b, h: (b, h)),
            pl.BlockSpec((w, seq), lambda b, h: (h, b)),
            pl.BlockSpec((n_blocks, w), lambda b, h: (b, h)),
        ],
        out_specs=pl.BlockSpec((seq, w), lambda b, h: (b, h)),
        out_shape=jax.ShapeDtypeStruct((batch * seq, d), BF16),
        scratch_shapes=[pltpu.VMEM((ATTN_LOOKAHEAD + 1, seq, MOBA_BLOCK), F32),
                        pltpu.VMEM((2, seq, MOBA_BLOCK), BF16)],
        compiler_params=_params("parallel", "arbitrary"),
        name="moba_attn",
    )(kq, kq, vt, km)


def kernel(x, a_norm, a_w_in, a_ln_g, a_ln_b, a_w_s, a_b_s, a_w_out, kv_norm, w_k, w_v, k_norm,
           b_norm, b_w_q, b_q_norm, b_w_o, ffn_norm, ffn_w_gate, ffn_w_up, ffn_w_down):
    batch, seq, d = x.shape
    assert seq % MOBA_BLOCK == 0 and seq // MOBA_BLOCK - 1 >= MOBA_TOPK
    n_a = a_norm.shape[0]
    depth = ffn_norm.shape[0]
    bf = lambda w: w.astype(BF16)

    h = x.reshape(batch * seq, d)
    kv = None
    for layer in range(depth):
        if layer < n_a:
            i = layer
            ug = _sgu(h, a_norm[i], a_w_in, i, a_ln_g[i], a_ln_b[i], a_w_s[i], a_b_s[i])
            h = _mm_res(ug, a_w_out, i, h)
        else:
            i = layer - n_a
            assert depth - n_a == 1
            w_kvq = jnp.concatenate([bf(w_k), bf(w_v), bf(b_w_q[i])], axis=1)
            kq, vt, km = _kvq(h, kv_norm, b_norm[i], w_kvq, k_norm, b_q_norm[i])
            attn = _attention(kq, vt, km, batch, seq, d)
            h = _mm_res(attn, b_w_o, i, h)
        h = _ffn(h, ffn_norm[layer], ffn_w_gate, ffn_w_up, ffn_w_down, layer)
    return h.reshape(batch, seq, d)
```

```python
import functools

import jax
import jax.numpy as jnp
from jax import lax
from jax.experimental import pallas as pl
from jax.experimental.pallas import tpu as pltpu

F32 = jnp.float32
BF16 = jnp.bfloat16
EPS = 1e-6

SGU_CHUNK = 128
SGU_GROUPS = 16
HEAD_DIM = 128
MOBA_BLOCK = 256
MOBA_TOPK = 3

VMEM_LIMIT_BYTES = 56 * 1024 * 1024
ROW_TILE = 1024
ROW_BLOCK = 256
NORM_ROWS = 128
FFN_ROW_TILE = 2048
FFN_ROW_BLOCK = 1024
MM_RES_ROWS = 512
LOG2_E = 1.4426950408889634
ATTN_HEADS_PER_STEP = 4
ATTN_LOOKAHEAD = 3


def _params(*sem):
    return pltpu.CompilerParams(dimension_semantics=sem, vmem_limit_bytes=VMEM_LIMIT_BYTES)


def _rows(n_rows, block, body, unroll=False):
    if unroll:
        for r in range(n_rows // block):
            body(r * block)
        return

    def step(r, carry):
        body(pl.multiple_of(r * block, block))
        return carry
    lax.fori_loop(0, n_rows // block, step, 0)


def _rms_scale(x):
    return x * lax.rsqrt(jnp.mean(x * x, axis=-1, keepdims=True) + EPS)


def _sgu_kernel(x_ref, g_ref, w_ref, lng_ref, lnb_ref, ws_ref, bs_ref, o_ref,
                h_scr, u_scr, v_scr, wm_scr, *, tm, tn, n_u, n_v):
    j = pl.program_id(1)
    last = n_u + n_v - 1
    d = n_v * tn
    c = SGU_CHUNK

    def mask_weights():
        t_idx = lax.broadcasted_iota(jnp.int32, (c, c), 0)
        s_idx = lax.broadcasted_iota(jnp.int32, (c, c), 1)
        for g in range(SGU_GROUPS):
            wm_scr[g] = jnp.where(s_idx <= t_idx, ws_ref[g], 0.0).astype(BF16)

    def norm(r0):
        for r in range(r0, r0 + ROW_BLOCK, NORM_ROWS):
            x = x_ref[r:r + NORM_ROWS, :]
            h_scr[r:r + NORM_ROWS, :] = (_rms_scale(x) * g_ref[...]).astype(BF16)

    def gelu_mm(r0):
        z = jnp.dot(h_scr[r0:r0 + ROW_BLOCK, :], w_ref[...].astype(BF16),
                    preferred_element_type=F32)
        return jax.nn.gelu(z, approximate=True)

    def gate_rows(r0):
        for r in range(r0, r0 + ROW_BLOCK, c):
            parts = [v_scr[t, r:r + c, :] for t in range(n_v)]
            mu = sum(jnp.sum(p, axis=-1, keepdims=True) for p in parts) / d
            cen = [p - mu for p in parts]
            var = sum(jnp.sum(x * x, axis=-1, keepdims=True) for x in cen) / d
            inv = lax.rsqrt(var + EPS)
            for t in range(n_v):
                cols = slice(t * tn, (t + 1) * tn)
                v_ln = (cen[t] * inv * lng_ref[:, cols] + lnb_ref[:, cols]).astype(BF16)
                for g in range(t * tn // c, (t + 1) * tn // c):
                    off = g * c - t * tn
                    mixed = jnp.dot(wm_scr[g], v_ln[:, off:off + c], preferred_element_type=F32)
                    gate = mixed + bs_ref[g]
                    u = u_scr[t, r:r + c, off:off + c].astype(F32)
                    o_ref[r:r + c, g * c:(g + 1) * c] = (u * gate).astype(o_ref.dtype)

    def trips(first_step, u_step, last_step):
        for r0 in range(0, tm, ROW_BLOCK):
            if first_step:
                norm(r0)
            z = gelu_mm(r0)
            if u_step:
                u_scr[j, r0:r0 + ROW_BLOCK, :] = z.astype(BF16)
            elif last_step:
                v_scr[n_v - 1, r0:r0 + ROW_BLOCK, :] = z
                gate_rows(r0)
            else:
                v_scr[j - n_u, r0:r0 + ROW_BLOCK, :] = z

    @pl.when(j == 0)
    def _():
        mask_weights()
        trips(True, True, False)

    @pl.when(jnp.logical_and(j > 0, j < n_u))
    def _():
        trips(False, True, False)

    @pl.when(jnp.logical_and(j >= n_u, j < last))
    def _():
        trips(False, False, False)

    @pl.when(j == last)
    def _():
        trips(False, False, True)


def _sgu(x2d, norm_g, w_in, layer, ln_g, ln_b, w_s, b_s):
    t, d = x2d.shape
    d_sgu = w_in.shape[2] // 2
    tm, tn = ROW_TILE, 512
    n_u = n_v = d_sgu // tn
    assert d_sgu == SGU_GROUPS * SGU_CHUNK and tn % SGU_CHUNK == 0
    kern = functools.partial(_sgu_kernel, tm=tm, tn=tn, n_u=n_u, n_v=n_v)
    return pl.pallas_call(
        kern,
        grid=(t // tm, n_u + n_v),
        in_specs=[
            pl.BlockSpec((tm, d), lambda i, j: (i, 0)),
            pl.BlockSpec((1, d), lambda i, j: (0, 0)),
            pl.BlockSpec((None, d, tn), lambda i, j: (layer, 0, j)),
            pl.BlockSpec((1, d_sgu), lambda i, j: (0, 0)),
            pl.BlockSpec((1, d_sgu), lambda i, j: (0, 0)),
            pl.BlockSpec(w_s.shape, lambda i, j: (0, 0, 0)),
            pl.BlockSpec((SGU_GROUPS, SGU_CHUNK, 1), lambda i, j: (0, 0, 0)),
        ],
        out_specs=pl.BlockSpec((tm, d_sgu), lambda i, j: (i, 0)),
        out_shape=jax.ShapeDtypeStruct((t, d_sgu), BF16),
        scratch_shapes=[pltpu.VMEM((tm, d), BF16), pltpu.VMEM((n_u, tm, tn), BF16),
                        pltpu.VMEM((n_v, tm, tn), F32),
                        pltpu.VMEM((SGU_GROUPS, SGU_CHUNK, SGU_CHUNK), BF16)],
        compiler_params=_params("parallel", "arbitrary"),
        name="sgu",
    )(x2d, norm_g.reshape(1, d), w_in, ln_g.reshape(1, d_sgu), ln_b.reshape(1, d_sgu),
      w_s, b_s.reshape(SGU_GROUPS, SGU_CHUNK, 1))


def _mm_res_kernel(a_ref, w_ref, x_ref, o_ref, *, tn):
    for c0 in range(0, o_ref.shape[1], tn):
        cols = slice(c0, c0 + tn)
        acc = jnp.dot(a_ref[...], w_ref[:, cols].astype(BF16), preferred_element_type=F32)
        o_ref[:, cols] = x_ref[:, cols] + acc


def _mm_res(a, w_stack, layer, x2d):
    t, k = a.shape
    n = w_stack.shape[2]
    tm = MM_RES_ROWS
    return pl.pallas_call(
        functools.partial(_mm_res_kernel, tn=512),
        grid=(t // tm,),
        in_specs=[
            pl.BlockSpec((tm, k), lambda i: (i, 0)),
            pl.BlockSpec((None, k, n), lambda i: (layer, 0, 0), pipeline_mode=pl.Buffered(1)),
            pl.BlockSpec((tm, n), lambda i: (i, 0)),
        ],
        out_specs=pl.BlockSpec((tm, n), lambda i: (i, 0)),
        out_shape=jax.ShapeDtypeStruct((t, n), F32),
        compiler_params=_params("parallel"),
        name="mm_res",
    )(a, w_stack, x2d)


def _ffn_kernel(x_hbm, g_ref, wg_ref, wu_ref, wd_ref, o_hbm, acc, h_scr, in_sem, out_sem,
                *, tm, rb):
    i = pl.program_id(0)
    j = pl.program_id(1)
    n_i = pl.num_programs(0)
    n_j = pl.num_programs(1)
    n_blk = tm // rb

    def in_copy(c):
        return pltpu.make_async_copy(x_hbm.at[pl.ds(i * tm + c * rb, rb), :],
                                     acc.at[pl.ds(c * rb, rb), :], in_sem.at[c])

    def out_copy(tile, c):
        return pltpu.make_async_copy(acc.at[pl.ds(c * rb, rb), :],
                                     o_hbm.at[pl.ds(tile * tm + c * rb, rb), :], out_sem.at[c])

    @pl.when(j == 0)
    def _():
        for c in range(n_blk):
            @pl.when(i > 0)
            def _():
                out_copy(i - 1, c).wait()
            in_copy(c).start()
        for c in range(n_blk):
            in_copy(c).wait()

            def norm(r0):
                rows = pl.ds(pl.multiple_of(c * rb + r0, NORM_ROWS), NORM_ROWS)
                h_scr[rows, :] = (_rms_scale(acc[rows, :]) * g_ref[...]).astype(BF16)
            _rows(rb, NORM_ROWS, norm)

    def trips(write_back):
        for r in range(n_blk):
            rows = pl.ds(r * rb, rb)
            hb = h_scr[rows, :]
            gate = jnp.dot(hb, wg_ref[...].astype(BF16), preferred_element_type=F32)
            up = jnp.dot(hb, wu_ref[...].astype(BF16), preferred_element_type=F32)
            act = (jax.nn.silu(gate) * up).astype(BF16)
            acc[rows, :] += jnp.dot(act, wd_ref[...].astype(BF16), preferred_element_type=F32)
            if write_back:
                out_copy(i, r).start()

    @pl.when(j < n_j - 1)
    def _():
        trips(False)

    @pl.when(j == n_j - 1)
    def _():
        trips(True)

    @pl.when(jnp.logical_and(i == n_i - 1, j == n_j - 1))
    def _():
        for c in range(n_blk):
            out_copy(i, c).wait()


def _ffn(x2d, norm_g, w_gate, w_up, w_down, layer):
    t, d = x2d.shape
    f = w_gate.shape[2]
    tm, tf, rb = FFN_ROW_TILE, 512, FFN_ROW_BLOCK
    return pl.pallas_call(
        functools.partial(_ffn_kernel, tm=tm, rb=rb),
        grid=(t // tm, f // tf),
        in_specs=[
            pl.BlockSpec(memory_space=pl.ANY),
            pl.BlockSpec((1, d), lambda i, j: (0, 0)),
            pl.BlockSpec((None, d, tf), lambda i, j: (layer, 0, j)),
            pl.BlockSpec((None, d, tf), lambda i, j: (layer, 0, j)),
            pl.BlockSpec((None, tf, d), lambda i, j: (layer, j, 0)),
        ],
        out_specs=pl.BlockSpec(memory_space=pl.ANY),
        out_shape=jax.ShapeDtypeStruct((t, d), F32),
        scratch_shapes=[pltpu.VMEM((tm, d), F32), pltpu.VMEM((tm, d), BF16),
                        pltpu.SemaphoreType.DMA((tm // rb,)), pltpu.SemaphoreType.DMA((tm // rb,))],
        compiler_params=_params("arbitrary", "arbitrary"),
        name="ffn",
    )(x2d, norm_g.reshape(1, d), w_gate, w_up, w_down)


def _kvq_kernel(x_ref, gkv_ref, gq_ref, w_ref, nk_ref, nq_ref, kq_ref, vt_ref, km_ref,
                hkv_scr, hq_scr, *, tm, tn, n_proj, q_scale):
    j = pl.program_id(1)
    heads = tn // HEAD_DIM
    blocks = tm // MOBA_BLOCK

    def norm(blk):
        for r in range(blk * MOBA_BLOCK, (blk + 1) * MOBA_BLOCK, NORM_ROWS):
            y = _rms_scale(x_ref[r:r + NORM_ROWS, :])
            hkv_scr[r:r + NORM_ROWS, :] = (y * gkv_ref[...]).astype(BF16)
            hq_scr[r:r + NORM_ROWS, :] = (y * gq_ref[...]).astype(BF16)

    def proj(h_scr, blk):
        rows = slice(blk * MOBA_BLOCK, (blk + 1) * MOBA_BLOCK)
        return rows, jnp.dot(h_scr[rows, :], w_ref[...].astype(BF16), preferred_element_type=F32)

    def k_trips(first_step):
        for blk in range(blocks):
            if first_step:
                norm(blk)
            rows, y = proj(hkv_scr, blk)
            for hh in range(heads):
                cols = slice(hh * HEAD_DIM, (hh + 1) * HEAD_DIM)
                kh = _rms_scale(y[:, cols]) * nk_ref[...]
                kq_ref[rows, cols] = kh.astype(kq_ref.dtype)
                km_ref[0, blk:blk + 1, cols] = jnp.mean(kh, axis=0, keepdims=True)

    @pl.when(j == 0)
    def _():
        k_trips(True)

    @pl.when(jnp.logical_and(j > 0, j < n_proj))
    def _():
        k_trips(False)

    @pl.when(jnp.logical_and(j >= n_proj, j < 2 * n_proj))
    def _():
        for blk in range(blocks):
            rows, y = proj(hkv_scr, blk)
            vt_ref[:, rows] = y.T.astype(vt_ref.dtype)

    @pl.when(j >= 2 * n_proj)
    def _():
        for blk in range(blocks):
            rows, y = proj(hq_scr, blk)
            for hh in range(heads):
                cols = slice(hh * HEAD_DIM, (hh + 1) * HEAD_DIM)
                qh = _rms_scale(y[:, cols]) * nq_ref[...] * q_scale
                kq_ref[rows, cols] = qh.astype(kq_ref.dtype)


def _kvq(x2d, kv_norm, q_in_norm, w_kvq, k_norm, q_norm):
    t, d = x2d.shape
    tm, tn = ROW_TILE, 1024
    n_proj = d // tn
    blocks = tm // MOBA_BLOCK
    kern = functools.partial(_kvq_kernel, tm=tm, tn=tn, n_proj=n_proj, q_scale=HEAD_DIM ** -0.5 * LOG2_E)

    def kq_map(i, j):
        return i, jnp.where(j < n_proj, j, jnp.maximum(j - n_proj, n_proj - 1))

    kq, vt, km = pl.pallas_call(
        kern,
        grid=(t // tm, 3 * n_proj),
        in_specs=[
            pl.BlockSpec((tm, d), lambda i, j: (i, 0)),
            pl.BlockSpec((1, d), lambda i, j: (0, 0)),
            pl.BlockSpec((1, d), lambda i, j: (0, 0)),
            pl.BlockSpec((d, tn), lambda i, j: (0, j)),
            pl.BlockSpec((1, HEAD_DIM), lambda i, j: (0, 0)),
            pl.BlockSpec((1, HEAD_DIM), lambda i, j: (0, 0)),
        ],
        out_specs=[
            pl.BlockSpec((tm, tn), kq_map),
            pl.BlockSpec((tn, tm), lambda i, j: (jnp.clip(j - n_proj, 0, n_proj - 1), i)),
            pl.BlockSpec((1, blocks, tn), lambda i, j: (i, 0, jnp.minimum(j, n_proj - 1))),
        ],
        out_shape=[jax.ShapeDtypeStruct((t, 2 * d), BF16),
                   jax.ShapeDtypeStruct((d, t), BF16),
                   jax.ShapeDtypeStruct((t // tm, blocks, d), F32)],
        scratch_shapes=[pltpu.VMEM((tm, d), BF16), pltpu.VMEM((tm, d), BF16)],
        compiler_params=_params("parallel", "arbitrary"),
        name="kvq",
    )(x2d, kv_norm.reshape(1, d), q_in_norm.reshape(1, d), w_kvq,
      k_norm.reshape(1, HEAD_DIM), q_norm.reshape(1, HEAD_DIM))
    return kq, vt, km.reshape(t // MOBA_BLOCK, d)


_NT = (((1,), (1,)), ((), ()))


def _attn_kernel(q_ref, k_ref, vt_ref, km_ref, o_ref, s_scr, p_scr, *, n_blocks):
    bs = MOBA_BLOCK
    sub = 8
    grp = bs // sub
    dh = HEAD_DIM
    n_heads = q_ref.shape[1] // dh
    blk_id = lax.broadcasted_iota(jnp.int32, (n_blocks, bs), 0)
    k_pos = lax.broadcasted_iota(jnp.int32, (bs, bs), 0)
    q_pos = lax.broadcasted_iota(jnp.int32, (bs, bs), 1)
    neg_inf = -jnp.inf
    causal = jnp.where(k_pos <= q_pos, 0.0, neg_inf).reshape(grp, sub, bs)

    n_slots = s_scr.shape[0]

    def scores(job):
        hd, i = jobs[job]
        cols = slice(hd * dh, (hd + 1) * dh)
        q = q_ref[i * bs:(i + 1) * bs, cols]
        nk = (i + 1) * bs
        s_scr[job % n_slots, 0:nk, :] = lax.dot_general(k_ref[0:nk, cols], q, _NT,
                                                        preferred_element_type=F32)
        if i == 0:
            return None
        km = km_ref[:, cols].astype(BF16)
        return lax.dot_general(km, q, _NT, preferred_element_type=F32)

    jobs = [(hd, i) for i in range(n_blocks - 1, -1, -1) for hd in range(n_heads)]
    gates = {job: scores(job) for job in range(min(ATTN_LOOKAHEAD, len(jobs)))}
    for job, (hd, i) in enumerate(jobs):
        cols = slice(hd * dh, (hd + 1) * dh)
        nk = (i + 1) * bs
        s_ref = s_scr.at[job % n_slots]
        p_ref = p_scr.at[job % 2]
        if job + ATTN_LOOKAHEAD < len(jobs):
            gates[job + ATTN_LOOKAHEAD] = scores(job + ATTN_LOOKAHEAD)
        gate = gates.pop(job)

        def piece(jb):
            return s_ref[jb * bs:(jb + 1) * bs, :].reshape(grp, sub, bs)

        biases = []
        if i > 0:
            rank = jnp.zeros((n_blocks, bs), F32)
            for m in range(i):
                gm = jnp.broadcast_to(gate[m:m + 1, :], (n_blocks, bs))
                beats = jnp.logical_or(gm > gate, jnp.logical_and(gm == gate, blk_id > m))
                rank = rank + jnp.where(beats, 1.0, 0.0)
            sel = jnp.logical_and(blk_id < i, rank < MOBA_TOPK)
            bias = jnp.where(sel, 0.0, neg_inf)
            biases = [jnp.broadcast_to(bias[jb:jb + 1, :], (sub, bs)) for jb in range(i)]
        m8 = jnp.max(piece(i) + causal, axis=0)
        for jb in range(i):
            m8 = jnp.maximum(m8, jnp.max(piece(jb), axis=0) + biases[jb])
        m_b = jnp.broadcast_to(jnp.max(m8, axis=0, keepdims=True), (sub, bs))
        l8 = jnp.zeros((sub, bs), F32)
        for jb in range(i + 1):
            if jb < i:
                p = jnp.exp2(piece(jb) - (m_b - biases[jb])[None])
            else:
                p = jnp.exp2(piece(jb) + causal - m_b[None])
            l8 = l8 + jnp.sum(p, axis=0)
            p_ref[jb * bs:(jb + 1) * bs, :] = p.reshape(bs, bs).astype(BF16)
        denom = jnp.sum(l8, axis=0, keepdims=True)
        o_t = jnp.dot(vt_ref[cols, 0:nk], p_ref[0:nk, :], preferred_element_type=F32)
        o_ref[i * bs:(i + 1) * bs, cols] = (o_t / denom).T.astype(o_ref.dtype)


def _attention(kq, vt, km, batch, seq, d):
    n_blocks = seq // MOBA_BLOCK
    w = ATTN_HEADS_PER_STEP * HEAD_DIM
    n_groups = d // w
    return pl.pallas_call(
        functools.partial(_attn_kernel, n_blocks=n_blocks),
        grid=(batch, n_groups),
        in_specs=[
            pl.BlockSpec((seq, w), lambda b, h: (b, n_groups + h)),
            pl.BlockSpec((seq, w), lambda b, h: (b, h)),
            pl.BlockSpec((w, seq), lambda b, h: (h, b)),
            pl.BlockSpec((n_blocks, w), lambda b, h: (b, h)),
        ],
        out_specs=pl.BlockSpec((seq, w), lambda b, h: (b, h)),
        out_shape=jax.ShapeDtypeStruct((batch * seq, d), BF16),
        scratch_shapes=[pltpu.VMEM((ATTN_LOOKAHEAD + 1, seq, MOBA_BLOCK), F32),
                        pltpu.VMEM((2, seq, MOBA_BLOCK), BF16)],
        compiler_params=_params("parallel", "arbitrary"),
        name="moba_attn",
    )(kq, kq, vt, km)


def kernel(x, a_norm, a_w_in, a_ln_g, a_ln_b, a_w_s, a_b_s, a_w_out, kv_norm, w_k, w_v, k_norm,
           b_norm, b_w_q, b_q_norm, b_w_o, ffn_norm, ffn_w_gate, ffn_w_up, ffn_w_down):
    batch, seq, d = x.shape
    assert seq % MOBA_BLOCK == 0 and seq // MOBA_BLOCK - 1 >= MOBA_TOPK
    n_a = a_norm.shape[0]
    depth = ffn_norm.shape[0]

    h = x.reshape(batch * seq, d)
    for layer in range(depth):
        if layer < n_a:
            i = layer
            ug = _sgu(h, a_norm[i], a_w_in, i, a_ln_g[i], a_ln_b[i], a_w_s[i], a_b_s[i])
            h = _mm_res(ug, a_w_out, i, h)
        else:
            i = layer - n_a
            assert depth - n_a == 1
            w_kvq = jnp.concatenate([w_k, w_v, b_w_q[i]], axis=1).astype(BF16)
            kq, vt, km = _kvq(h, kv_norm, b_norm[i], w_kvq, k_norm, b_q_norm[i])
            attn = _attention(kq, vt, km, batch, seq, d)
            h = _mm_res(attn, b_w_o, i, h)
        h = _ffn(h, ffn_norm[layer], ffn_w_gate, ffn_w_up, ffn_w_down, layer)
    return h.reshape(batch, seq, d)
```

```python
import functools

import jax
import jax.numpy as jnp
from jax import lax
from jax.experimental import pallas as pl
from jax.experimental.pallas import tpu as pltpu

F32 = jnp.float32
BF16 = jnp.bfloat16
EPS = 1e-6

SGU_CHUNK = 128
SGU_GROUPS = 16
HEAD_DIM = 128
MOBA_BLOCK = 256
MOBA_TOPK = 3

VMEM_LIMIT_BYTES = 56 * 1024 * 1024
KVQ_VMEM_LIMIT_BYTES = 60 * 1024 * 1024
ROW_TILE = 1024
ROW_BLOCK = 256
NORM_ROWS = 128
FFN_ROW_TILE = 2048
FFN_ROW_BLOCK = 1024
MM_RES_ROWS = 512
LOG2_E = 1.4426950408889634
ATTN_HEADS_PER_STEP = 4
ATTN_LOOKAHEAD = 3


def _params(*sem, vmem_limit_bytes=VMEM_LIMIT_BYTES):
    return pltpu.CompilerParams(dimension_semantics=sem, vmem_limit_bytes=vmem_limit_bytes)


def _rows(n_rows, block, body, unroll=False):
    if unroll:
        for r in range(n_rows // block):
            body(r * block)
        return

    def step(r, carry):
        body(pl.multiple_of(r * block, block))
        return carry
    lax.fori_loop(0, n_rows // block, step, 0)


def _rms_scale(x):
    return x * lax.rsqrt(jnp.mean(x * x, axis=-1, keepdims=True) + EPS)


def _sgu_kernel(x_ref, g_ref, w_ref, lng_ref, lnb_ref, ws_ref, bs_ref, o_ref,
                h_scr, u_scr, v_scr, wm_scr, *, tm, tn, n_u, n_v):
    j = pl.program_id(1)
    last = n_u + n_v - 1
    d = n_v * tn
    c = SGU_CHUNK

    def mask_weights():
        t_idx = lax.broadcasted_iota(jnp.int32, (c, c), 0)
        s_idx = lax.broadcasted_iota(jnp.int32, (c, c), 1)
        for g in range(SGU_GROUPS):
            wm_scr[g] = jnp.where(s_idx <= t_idx, ws_ref[g], 0.0).astype(BF16)

    def norm(r0):
        for r in range(r0, r0 + ROW_BLOCK, NORM_ROWS):
            x = x_ref[r:r + NORM_ROWS, :]
            h_scr[r:r + NORM_ROWS, :] = (_rms_scale(x) * g_ref[...]).astype(BF16)

    def gelu_mm(r0):
        z = jnp.dot(h_scr[r0:r0 + ROW_BLOCK, :], w_ref[...].astype(BF16),
                    preferred_element_type=F32)
        return jax.nn.gelu(z, approximate=True)

    def gate_rows(r0):
        for r in range(r0, r0 + ROW_BLOCK, c):
            parts = [v_scr[t, r:r + c, :] for t in range(n_v)]
            mu = sum(jnp.sum(p, axis=-1, keepdims=True) for p in parts) / d
            cen = [p - mu for p in parts]
            var = sum(jnp.sum(x * x, axis=-1, keepdims=True) for x in cen) / d
            inv = lax.rsqrt(var + EPS)
            for t in range(n_v):
                cols = slice(t * tn, (t + 1) * tn)
                v_ln = (cen[t] * inv * lng_ref[:, cols] + lnb_ref[:, cols]).astype(BF16)
                for g in range(t * tn // c, (t + 1) * tn // c):
                    off = g * c - t * tn
                    mixed = jnp.dot(wm_scr[g], v_ln[:, off:off + c], preferred_element_type=F32)
                    gate = mixed + bs_ref[g]
                    u = u_scr[t, r:r + c, off:off + c].astype(F32)
                    o_ref[r:r + c, g * c:(g + 1) * c] = (u * gate).astype(o_ref.dtype)

    def trips(first_step, u_step, last_step):
        for r0 in range(0, tm, ROW_BLOCK):
            if first_step:
                norm(r0)
            z = gelu_mm(r0)
            if u_step:
                u_scr[j, r0:r0 + ROW_BLOCK, :] = z.astype(BF16)
            elif last_step:
                v_scr[n_v - 1, r0:r0 + ROW_BLOCK, :] = z
                gate_rows(r0)
            else:
                v_scr[j - n_u, r0:r0 + ROW_BLOCK, :] = z

    @pl.when(j == 0)
    def _():
        mask_weights()
        trips(True, True, False)

    @pl.when(jnp.logical_and(j > 0, j < n_u))
    def _():
        trips(False, True, False)

    @pl.when(jnp.logical_and(j >= n_u, j < last))
    def _():
        trips(False, False, False)

    @pl.when(j == last)
    def _():
        trips(False, False, True)


def _sgu(x2d, norm_g, w_in, layer, ln_g, ln_b, w_s, b_s):
    t, d = x2d.shape
    d_sgu = w_in.shape[2] // 2
    tm, tn = ROW_TILE, 512
    n_u = n_v = d_sgu // tn
    assert d_sgu == SGU_GROUPS * SGU_CHUNK and tn % SGU_CHUNK == 0
    kern = functools.partial(_sgu_kernel, tm=tm, tn=tn, n_u=n_u, n_v=n_v)
    return pl.pallas_call(
        kern,
        grid=(t // tm, n_u + n_v),
        in_specs=[
            pl.BlockSpec((tm, d), lambda i, j: (i, 0)),
            pl.BlockSpec((1, d), lambda i, j: (0, 0)),
            pl.BlockSpec((None, d, tn), lambda i, j: (layer, 0, j)),
            pl.BlockSpec((1, d_sgu), lambda i, j: (0, 0)),
            pl.BlockSpec((1, d_sgu), lambda i, j: (0, 0)),
            pl.BlockSpec(w_s.shape, lambda i, j: (0, 0, 0)),
            pl.BlockSpec((SGU_GROUPS, SGU_CHUNK, 1), lambda i, j: (0, 0, 0)),
        ],
        out_specs=pl.BlockSpec((tm, d_sgu), lambda i, j: (i, 0)),
        out_shape=jax.ShapeDtypeStruct((t, d_sgu), BF16),
        scratch_shapes=[pltpu.VMEM((tm, d), BF16), pltpu.VMEM((n_u, tm, tn), BF16),
                        pltpu.VMEM((n_v, tm, tn), F32),
                        pltpu.VMEM((SGU_GROUPS, SGU_CHUNK, SGU_CHUNK), BF16)],
        compiler_params=_params("parallel", "arbitrary"),
        name="sgu",
    )(x2d, norm_g.reshape(1, d), w_in, ln_g.reshape(1, d_sgu), ln_b.reshape(1, d_sgu),
      w_s, b_s.reshape(SGU_GROUPS, SGU_CHUNK, 1))


def _mm_res_kernel(a_ref, w_ref, x_ref, o_ref, *, tn):
    for c0 in range(0, o_ref.shape[1], tn):
        cols = slice(c0, c0 + tn)
        acc = jnp.dot(a_ref[...], w_ref[:, cols].astype(BF16), preferred_element_type=F32)
        o_ref[:, cols] = x_ref[:, cols] + acc


def _mm_res(a, w_stack, layer, x2d):
    t, k = a.shape
    n = w_stack.shape[2]
    tm = MM_RES_ROWS
    return pl.pallas_call(
        functools.partial(_mm_res_kernel, tn=512),
        grid=(t // tm,),
        in_specs=[
            pl.BlockSpec((tm, k), lambda i: (i, 0)),
            pl.BlockSpec((None, k, n), lambda i: (layer, 0, 0), pipeline_mode=pl.Buffered(1)),
            pl.BlockSpec((tm, n), lambda i: (i, 0)),
        ],
        out_specs=pl.BlockSpec((tm, n), lambda i: (i, 0)),
        out_shape=jax.ShapeDtypeStruct((t, n), F32),
        compiler_params=_params("parallel"),
        name="mm_res",
    )(a, w_stack, x2d)


def _ffn_kernel(x_hbm, g_ref, wg_ref, wu_ref, wd_ref, o_hbm, acc, h_scr, in_sem, out_sem,
                *, tm, rb):
    i = pl.program_id(0)
    j = pl.program_id(1)
    n_i = pl.num_programs(0)
    n_j = pl.num_programs(1)
    n_blk = tm // rb

    def in_copy(c):
        return pltpu.make_async_copy(x_hbm.at[pl.ds(i * tm + c * rb, rb), :],
                                     acc.at[pl.ds(c * rb, rb), :], in_sem.at[c])

    def out_copy(tile, c):
        return pltpu.make_async_copy(acc.at[pl.ds(c * rb, rb), :],
                                     o_hbm.at[pl.ds(tile * tm + c * rb, rb), :], out_sem.at[c])

    @pl.when(j == 0)
    def _():
        for c in range(n_blk):
            @pl.when(i > 0)
            def _():
                out_copy(i - 1, c).wait()
            in_copy(c).start()
        for c in range(n_blk):
            in_copy(c).wait()

            def norm(r0):
                rows = pl.ds(pl.multiple_of(c * rb + r0, NORM_ROWS), NORM_ROWS)
                h_scr[rows, :] = (_rms_scale(acc[rows, :]) * g_ref[...]).astype(BF16)
            _rows(rb, NORM_ROWS, norm)

    def trips(write_back):
        for r in range(n_blk):
            rows = pl.ds(r * rb, rb)
            hb = h_scr[rows, :]
            gate = jnp.dot(hb, wg_ref[...].astype(BF16), preferred_element_type=F32)
            up = jnp.dot(hb, wu_ref[...].astype(BF16), preferred_element_type=F32)
            act = (jax.nn.silu(gate) * up).astype(BF16)
            acc[rows, :] += jnp.dot(act, wd_ref[...].astype(BF16), preferred_element_type=F32)
            if write_back:
                out_copy(i, r).start()

    @pl.when(j < n_j - 1)
    def _():
        trips(False)

    @pl.when(j == n_j - 1)
    def _():
        trips(True)

    @pl.when(jnp.logical_and(i == n_i - 1, j == n_j - 1))
    def _():
        for c in range(n_blk):
            out_copy(i, c).wait()


def _ffn(x2d, norm_g, w_gate, w_up, w_down, layer):
    t, d = x2d.shape
    f = w_gate.shape[2]
    tm, tf, rb = FFN_ROW_TILE, 512, FFN_ROW_BLOCK
    return pl.pallas_call(
        functools.partial(_ffn_kernel, tm=tm, rb=rb),
        grid=(t // tm, f // tf),
        in_specs=[
            pl.BlockSpec(memory_space=pl.ANY),
            pl.BlockSpec((1, d), lambda i, j: (0, 0)),
            pl.BlockSpec((None, d, tf), lambda i, j: (layer, 0, j)),
            pl.BlockSpec((None, d, tf), lambda i, j: (layer, 0, j)),
            pl.BlockSpec((None, tf, d), lambda i, j: (layer, j, 0)),
        ],
        out_specs=pl.BlockSpec(memory_space=pl.ANY),
        out_shape=jax.ShapeDtypeStruct((t, d), F32),
        scratch_shapes=[pltpu.VMEM((tm, d), F32), pltpu.VMEM((tm, d), BF16),
                        pltpu.SemaphoreType.DMA((tm // rb,)), pltpu.SemaphoreType.DMA((tm // rb,))],
        compiler_params=_params("arbitrary", "arbitrary"),
        name="ffn",
    )(x2d, norm_g.reshape(1, d), w_gate, w_up, w_down)


def _kvq_kernel(x_ref, gkv_ref, gq_ref, wk_ref, wv_ref, wq_ref, nk_ref, nq_ref,
                kq_ref, vt_ref, km_ref, hkv_scr, hq_scr, *, tm, tn, n_proj, q_scale):
    j = pl.program_id(1)
    heads = tn // HEAD_DIM
    blocks = tm // MOBA_BLOCK

    def norm(blk):
        for r in range(blk * MOBA_BLOCK, (blk + 1) * MOBA_BLOCK, NORM_ROWS):
            y = _rms_scale(x_ref[r:r + NORM_ROWS, :])
            hkv_scr[r:r + NORM_ROWS, :] = (y * gkv_ref[...]).astype(BF16)
            hq_scr[r:r + NORM_ROWS, :] = (y * gq_ref[...]).astype(BF16)

    def proj(h_scr, w_ref, blk):
        rows = slice(blk * MOBA_BLOCK, (blk + 1) * MOBA_BLOCK)
        return rows, jnp.dot(h_scr[rows, :], w_ref[...].astype(BF16), preferred_element_type=F32)

    def k_trips(first_step):
        for blk in range(blocks):
            if first_step:
                norm(blk)
            rows, y = proj(hkv_scr, wk_ref, blk)
            for hh in range(heads):
                cols = slice(hh * HEAD_DIM, (hh + 1) * HEAD_DIM)
                kh = _rms_scale(y[:, cols]) * nk_ref[...]
                kq_ref[rows, cols] = kh.astype(kq_ref.dtype)
                km_ref[0, blk:blk + 1, cols] = jnp.mean(kh, axis=0, keepdims=True)

    @pl.when(j == 0)
    def _():
        k_trips(True)

    @pl.when(jnp.logical_and(j > 0, j < n_proj))
    def _():
        k_trips(False)

    @pl.when(jnp.logical_and(j >= n_proj, j < 2 * n_proj))
    def _():
        for blk in range(blocks):
            rows, y = proj(hkv_scr, wv_ref, blk)
            vt_ref[:, rows] = y.T.astype(vt_ref.dtype)

    @pl.when(j >= 2 * n_proj)
    def _():
        for blk in range(blocks):
            rows, y = proj(hq_scr, wq_ref, blk)
            for hh in range(heads):
                cols = slice(hh * HEAD_DIM, (hh + 1) * HEAD_DIM)
                qh = _rms_scale(y[:, cols]) * nq_ref[...] * q_scale
                kq_ref[rows, cols] = qh.astype(kq_ref.dtype)


def _kvq(x2d, kv_norm, q_in_norm, w_k, w_v, w_q, k_norm, q_norm):
    t, d = x2d.shape
    tm, tn = ROW_TILE, 1024
    n_proj = d // tn
    blocks = tm // MOBA_BLOCK
    kern = functools.partial(_kvq_kernel, tm=tm, tn=tn, n_proj=n_proj, q_scale=HEAD_DIM ** -0.5 * LOG2_E)

    def kq_map(i, j):
        return i, jnp.where(j < n_proj, j, jnp.maximum(j - n_proj, n_proj - 1))

    kq, vt, km = pl.pallas_call(
        kern,
        grid=(t // tm, 3 * n_proj),
        in_specs=[
            pl.BlockSpec((tm, d), lambda i, j: (i, 0)),
            pl.BlockSpec((1, d), lambda i, j: (0, 0)),
            pl.BlockSpec((1, d), lambda i, j: (0, 0)),
            pl.BlockSpec((d, tn), lambda i, j: (0, jnp.minimum(j, n_proj - 1))),
            pl.BlockSpec((d, tn), lambda i, j: (0, jnp.clip(j - n_proj, 0, n_proj - 1))),
            pl.BlockSpec((d, tn), lambda i, j: (0, jnp.clip(j - 2 * n_proj, 0, n_proj - 1))),
            pl.BlockSpec((1, HEAD_DIM), lambda i, j: (0, 0)),
            pl.BlockSpec((1, HEAD_DIM), lambda i, j: (0, 0)),
        ],
        out_specs=[
            pl.BlockSpec((tm, tn), kq_map),
            pl.BlockSpec((tn, tm), lambda i, j: (jnp.clip(j - n_proj, 0, n_proj - 1), i)),
            pl.BlockSpec((1, blocks, tn), lambda i, j: (i, 0, jnp.minimum(j, n_proj - 1))),
        ],
        out_shape=[jax.ShapeDtypeStruct((t, 2 * d), BF16),
                   jax.ShapeDtypeStruct((d, t), BF16),
                   jax.ShapeDtypeStruct((t // tm, blocks, d), F32)],
        scratch_shapes=[pltpu.VMEM((tm, d), BF16), pltpu.VMEM((tm, d), BF16)],
        compiler_params=_params("parallel", "arbitrary", vmem_limit_bytes=KVQ_VMEM_LIMIT_BYTES),
        name="kvq",
    )(x2d, kv_norm.reshape(1, d), q_in_norm.reshape(1, d), w_k, w_v, w_q,
      k_norm.reshape(1, HEAD_DIM), q_norm.reshape(1, HEAD_DIM))
    return kq, vt, km.reshape(t // MOBA_BLOCK, d)


_NT = (((1,), (1,)), ((), ()))


def _attn_kernel(q_ref, k_ref, vt_ref, km_ref, o_ref, s_scr, p_scr, *, n_blocks):
    bs = MOBA_BLOCK
    sub = 8
    grp = bs // sub
    dh = HEAD_DIM
    n_heads = q_ref.shape[1] // dh
    blk_id = lax.broadcasted_iota(jnp.int32, (n_blocks, bs), 0)
    k_pos = lax.broadcasted_iota(jnp.int32, (bs, bs), 0)
    q_pos = lax.broadcasted_iota(jnp.int32, (bs, bs), 1)
    neg_inf = -jnp.inf
    causal = jnp.where(k_pos <= q_pos, 0.0, neg_inf).reshape(grp, sub, bs)

    n_slots = s_scr.shape[0]

    def scores(job):
        hd, i = jobs[job]
        cols = slice(hd * dh, (hd + 1) * dh)
        q = q_ref[i * bs:(i + 1) * bs, cols]
        nk = (i + 1) * bs
        s_scr[job % n_slots, 0:nk, :] = lax.dot_general(k_ref[0:nk, cols], q, _NT,
                                                        preferred_element_type=F32)
        if i == 0:
            return None
        km = km_ref[:, cols].astype(BF16)
        return lax.dot_general(km, q, _NT, preferred_element_type=F32)

    jobs = [(hd, i) for i in range(n_blocks - 1, -1, -1) for hd in range(n_heads)]
    gates = {job: scores(job) for job in range(min(ATTN_LOOKAHEAD, len(jobs)))}
    for job, (hd, i) in enumerate(jobs):
        cols = slice(hd * dh, (hd + 1) * dh)
        nk = (i + 1) * bs
        s_ref = s_scr.at[job % n_slots]
        p_ref = p_scr.at[job % 2]
        if job + ATTN_LOOKAHEAD < len(jobs):
            gates[job + ATTN_LOOKAHEAD] = scores(job + ATTN_LOOKAHEAD)
        gate = gates.pop(job)

        def piece(jb):
            return s_ref[jb * bs:(jb + 1) * bs, :].reshape(grp, sub, bs)

        biases = []
        if i > 0:
            rank = jnp.zeros((n_blocks, bs), F32)
            for m in range(i):
                gm = jnp.broadcast_to(gate[m:m + 1, :], (n_blocks, bs))
                beats = jnp.logical_or(gm > gate, jnp.logical_and(gm == gate, blk_id > m))
                rank = rank + jnp.where(beats, 1.0, 0.0)
            sel = jnp.logical_and(blk_id < i, rank < MOBA_TOPK)
            bias = jnp.where(sel, 0.0, neg_inf)
            biases = [jnp.broadcast_to(bias[jb:jb + 1, :], (sub, bs)) for jb in range(i)]
        m8 = jnp.max(piece(i) + causal, axis=0)
        for jb in range(i):
            m8 = jnp.maximum(m8, jnp.max(piece(jb), axis=0) + biases[jb])
        m_b = jnp.broadcast_to(jnp.max(m8, axis=0, keepdims=True), (sub, bs))
        l8 = jnp.zeros((sub, bs), F32)
        for jb in range(i + 1):
            if jb < i:
                p = jnp.exp2(piece(jb) - (m_b - biases[jb])[None])
            else:
                p = jnp.exp2(piece(jb) + causal - m_b[None])
            l8 = l8 + jnp.sum(p, axis=0)
            p_ref[jb * bs:(jb + 1) * bs, :] = p.reshape(bs, bs).astype(BF16)
        denom = jnp.sum(l8, axis=0, keepdims=True)
        o_t = jnp.dot(vt_ref[cols, 0:nk], p_ref[0:nk, :], preferred_element_type=F32)
        o_ref[i * bs:(i + 1) * bs, cols] = (o_t / denom).T.astype(o_ref.dtype)


def _attention(kq, vt, km, batch, seq, d):
    n_blocks = seq // MOBA_BLOCK
    w = ATTN_HEADS_PER_STEP * HEAD_DIM
    n_groups = d // w
    return pl.pallas_call(
        functools.partial(_attn_kernel, n_blocks=n_blocks),
        grid=(batch, n_groups),
        in_specs=[
            pl.BlockSpec((seq, w), lambda b, h: (b, n_groups + h)),
            pl.BlockSpec((seq, w), lambda b, h: (b, h)),
            pl.BlockSpec((w, seq), lambda b, h: (h, b)),
            pl.BlockSpec((n_blocks, w), lambda b, h: (b, h)),
        ],
        out_specs=pl.BlockSpec((seq, w), lambda b, h: (b, h)),
        out_shape=jax.ShapeDtypeStruct((batch * seq, d), BF16),
        scratch_shapes=[pltpu.VMEM((ATTN_LOOKAHEAD + 1, seq, MOBA_BLOCK), F32),
                        pltpu.VMEM((2, seq, MOBA_BLOCK), BF16)],
        compiler_params=_params("parallel", "arbitrary"),
        name="moba_attn",
    )(kq, kq, vt, km)


def kernel(x, a_norm, a_w_in, a_ln_g, a_ln_b, a_w_s, a_b_s, a_w_out, kv_norm, w_k, w_v, k_norm,
           b_norm, b_w_q, b_q_norm, b_w_o, ffn_norm, ffn_w_gate, ffn_w_up, ffn_w_down):
    batch, seq, d = x.shape
    assert seq % MOBA_BLOCK == 0 and seq // MOBA_BLOCK - 1 >= MOBA_TOPK
    n_a = a_norm.shape[0]
    depth = ffn_norm.shape[0]

    h = x.reshape(batch * seq, d)
    for layer in range(depth):
        if layer < n_a:
            i = layer
            ug = _sgu(h, a_norm[i], a_w_in, i, a_ln_g[i], a_ln_b[i], a_w_s[i], a_b_s[i])
            h = _mm_res(ug, a_w_out, i, h)
        else:
            i = layer - n_a
            assert depth - n_a == 1
            kq, vt, km = _kvq(h, kv_norm, b_norm[i], w_k.astype(BF16), w_v.astype(BF16),
                              b_w_q[i].astype(BF16), k_norm, b_q_norm[i])
            attn = _attention(kq, vt, km, batch, seq, d)
            h = _mm_res(attn, b_w_o, i, h)
        h = _ffn(h, ffn_norm[layer], ffn_w_gate, ffn_w_up, ffn_w_down, layer)
    return h.reshape(batch, seq, d)
```

```python
import functools

import jax
import jax.numpy as jnp
from jax import lax
from jax.experimental import pallas as pl
from jax.experimental.pallas import tpu as pltpu

F32 = jnp.float32
BF16 = jnp.bfloat16
EPS = 1e-6

SGU_CHUNK = 128
SGU_GROUPS = 16
HEAD_DIM = 128
MOBA_BLOCK = 256
MOBA_TOPK = 3

VMEM_LIMIT_BYTES = 56 * 1024 * 1024
ROW_TILE = 1024
ROW_BLOCK = 256
NORM_ROWS = 128
FFN_ROW_TILE = 2048
FFN_ROW_BLOCK = 512
MM_RES_ROWS = 512
LOG2_E = 1.4426950408889634
ATTN_HEADS_PER_STEP = 4
ATTN_LOOKAHEAD = 3


def _params(*sem):
    return pltpu.CompilerParams(dimension_semantics=sem, vmem_limit_bytes=VMEM_LIMIT_BYTES)


def _rows(n_rows, block, body, unroll=False):
    if unroll:
        for r in range(n_rows // block):
            body(r * block)
        return

    def step(r, carry):
        body(pl.multiple_of(r * block, block))
        return carry
    lax.fori_loop(0, n_rows // block, step, 0)


def _rms_scale(x):
    return x * lax.rsqrt(jnp.mean(x * x, axis=-1, keepdims=True) + EPS)


def _sgu_kernel(x_ref, g_ref, w_ref, lng_ref, lnb_ref, ws_ref, bs_ref, o_ref,
                h_scr, u_scr, v_scr, wm_scr, *, tm, tn, n_u, n_v):
    j = pl.program_id(1)
    last = n_u + n_v - 1
    d = n_v * tn
    c = SGU_CHUNK

    def mask_weights():
        t_idx = lax.broadcasted_iota(jnp.int32, (c, c), 0)
        s_idx = lax.broadcasted_iota(jnp.int32, (c, c), 1)
        for g in range(SGU_GROUPS):
            wm_scr[g] = jnp.where(s_idx <= t_idx, ws_ref[g], 0.0).astype(BF16)

    def norm(r0):
        for r in range(r0, r0 + ROW_BLOCK, NORM_ROWS):
            x = x_ref[r:r + NORM_ROWS, :]
            h_scr[r:r + NORM_ROWS, :] = (_rms_scale(x) * g_ref[...]).astype(BF16)

    def gelu_mm(r0):
        z = jnp.dot(h_scr[r0:r0 + ROW_BLOCK, :], w_ref[...].astype(BF16),
                    preferred_element_type=F32)
        return jax.nn.gelu(z, approximate=True)

    def gate_rows(r0):
        for r in range(r0, r0 + ROW_BLOCK, c):
            parts = [v_scr[t, r:r + c, :] for t in range(n_v)]
            mu = sum(jnp.sum(p, axis=-1, keepdims=True) for p in parts) / d
            cen = [p - mu for p in parts]
            var = sum(jnp.sum(x * x, axis=-1, keepdims=True) for x in cen) / d
            inv = lax.rsqrt(var + EPS)
            for t in range(n_v):
                cols = slice(t * tn, (t + 1) * tn)
                v_ln = (cen[t] * inv * lng_ref[:, cols] + lnb_ref[:, cols]).astype(BF16)
                for g in range(t * tn // c, (t + 1) * tn // c):
                    off = g * c - t * tn
                    mixed = jnp.dot(wm_scr[g], v_ln[:, off:off + c], preferred_element_type=F32)
                    gate = mixed + bs_ref[g]
                    u = u_scr[t, r:r + c, off:off + c].astype(F32)
                    o_ref[r:r + c, g * c:(g + 1) * c] = (u * gate).astype(o_ref.dtype)

    def trips(first_step, u_step, last_step):
        for r0 in range(0, tm, ROW_BLOCK):
            if first_step:
                norm(r0)
            z = gelu_mm(r0)
            if u_step:
                u_scr[j, r0:r0 + ROW_BLOCK, :] = z.astype(BF16)
            elif last_step:
                v_scr[n_v - 1, r0:r0 + ROW_BLOCK, :] = z
                gate_rows(r0)
            else:
                v_scr[j - n_u, r0:r0 + ROW_BLOCK, :] = z

    @pl.when(j == 0)
    def _():
        mask_weights()
        trips(True, True, False)

    @pl.when(jnp.logical_and(j > 0, j < n_u))
    def _():
        trips(False, True, False)

    @pl.when(jnp.logical_and(j >= n_u, j < last))
    def _():
        trips(False, False, False)

    @pl.when(j == last)
    def _():
        trips(False, False, True)


def _sgu(x2d, norm_g, w_in, layer, ln_g, ln_b, w_s, b_s):
    t, d = x2d.shape
    d_sgu = w_in.shape[2] // 2
    tm, tn = ROW_TILE, 512
    n_u = n_v = d_sgu // tn
    assert d_sgu == SGU_GROUPS * SGU_CHUNK and tn % SGU_CHUNK == 0
    kern = functools.partial(_sgu_kernel, tm=tm, tn=tn, n_u=n_u, n_v=n_v)
    return pl.pallas_call(
        kern,
        grid=(t // tm, n_u + n_v),
        in_specs=[
            pl.BlockSpec((tm, d), lambda i, j: (i, 0)),
            pl.BlockSpec((1, d), lambda i, j: (0, 0)),
            pl.BlockSpec((None, d, tn), lambda i, j: (layer, 0, j)),
            pl.BlockSpec((1, d_sgu), lambda i, j: (0, 0)),
            pl.BlockSpec((1, d_sgu), lambda i, j: (0, 0)),
            pl.BlockSpec(w_s.shape, lambda i, j: (0, 0, 0)),
            pl.BlockSpec((SGU_GROUPS, SGU_CHUNK, 1), lambda i, j: (0, 0, 0)),
        ],
        out_specs=pl.BlockSpec((tm, d_sgu), lambda i, j: (i, 0)),
        out_shape=jax.ShapeDtypeStruct((t, d_sgu), BF16),
        scratch_shapes=[pltpu.VMEM((tm, d), BF16), pltpu.VMEM((n_u, tm, tn), BF16),
                        pltpu.VMEM((n_v, tm, tn), F32),
                        pltpu.VMEM((SGU_GROUPS, SGU_CHUNK, SGU_CHUNK), BF16)],
        compiler_params=_params("parallel", "arbitrary"),
        name="sgu",
    )(x2d, norm_g.reshape(1, d), w_in, ln_g.reshape(1, d_sgu), ln_b.reshape(1, d_sgu),
      w_s, b_s.reshape(SGU_GROUPS, SGU_CHUNK, 1))


def _mm_res_kernel(a_ref, w_ref, x_ref, o_ref, *, tn):
    for c0 in range(0, o_ref.shape[1], tn):
        cols = slice(c0, c0 + tn)
        acc = jnp.dot(a_ref[...], w_ref[:, cols].astype(BF16), preferred_element_type=F32)
        o_ref[:, cols] = x_ref[:, cols] + acc


def _mm_res(a, w_stack, layer, x2d):
    t, k = a.shape
    n = w_stack.shape[2]
    tm = MM_RES_ROWS
    return pl.pallas_call(
        functools.partial(_mm_res_kernel, tn=512),
        grid=(t // tm,),
        in_specs=[
            pl.BlockSpec((tm, k), lambda i: (i, 0)),
            pl.BlockSpec((None, k, n), lambda i: (layer, 0, 0), pipeline_mode=pl.Buffered(1)),
            pl.BlockSpec((tm, n), lambda i: (i, 0)),
        ],
        out_specs=pl.BlockSpec((tm, n), lambda i: (i, 0)),
        out_shape=jax.ShapeDtypeStruct((t, n), F32),
        compiler_params=_params("parallel"),
        name="mm_res",
    )(a, w_stack, x2d)


def _ffn_kernel(x_hbm, g_ref, wg_ref, wu_ref, wd_ref, o_hbm, acc, h_scr, in_sem, out_sem,
                *, tm, rb):
    i = pl.program_id(0)
    j = pl.program_id(1)
    n_i = pl.num_programs(0)
    n_j = pl.num_programs(1)
    n_blk = tm // rb

    def in_copy(tile, c):
        return pltpu.make_async_copy(x_hbm.at[pl.ds(tile * tm + c * rb, rb), :],
                                     acc.at[pl.ds(c * rb, rb), :], in_sem.at[c])

    def out_copy(tile, c):
        return pltpu.make_async_copy(acc.at[pl.ds(c * rb, rb), :],
                                     o_hbm.at[pl.ds(tile * tm + c * rb, rb), :], out_sem.at[c])

    @pl.when(j == 0)
    def _():
        @pl.when(i == 0)
        def _():
            for c in range(n_blk):
                in_copy(0, c).start()

        @pl.when(i > 0)
        def _():
            out_copy(i - 1, n_blk - 1).wait()
            in_copy(i, n_blk - 1).start()

        for c in range(n_blk):
            in_copy(i, c).wait()

            def norm(r0):
                rows = pl.ds(pl.multiple_of(c * rb + r0, NORM_ROWS), NORM_ROWS)
                h_scr[rows, :] = (_rms_scale(acc[rows, :]) * g_ref[...]).astype(BF16)
            _rows(rb, NORM_ROWS, norm)

    def trips(write_back):
        for r in range(n_blk):
            rows = pl.ds(r * rb, rb)
            hb = h_scr[rows, :]
            gate = jnp.dot(hb, wg_ref[...].astype(BF16), preferred_element_type=F32)
            up = jnp.dot(hb, wu_ref[...].astype(BF16), preferred_element_type=F32)
            act = (jax.nn.silu(gate) * up).astype(BF16)
            acc[rows, :] += jnp.dot(act, wd_ref[...].astype(BF16), preferred_element_type=F32)
            if write_back:
                out_copy(i, r).start()
                if r > 0:
                    @pl.when(i < n_i - 1)
                    def _(r=r):
                        out_copy(i, r - 1).wait()
                        in_copy(i + 1, r - 1).start()

    @pl.when(j < n_j - 1)
    def _():
        trips(False)

    @pl.when(j == n_j - 1)
    def _():
        trips(True)

    @pl.when(jnp.logical_and(i == n_i - 1, j == n_j - 1))
    def _():
        for c in range(n_blk):
            out_copy(i, c).wait()


def _ffn(x2d, norm_g, w_gate, w_up, w_down, layer):
    t, d = x2d.shape
    f = w_gate.shape[2]
    tm, tf, rb = FFN_ROW_TILE, 512, FFN_ROW_BLOCK
    return pl.pallas_call(
        functools.partial(_ffn_kernel, tm=tm, rb=rb),
        grid=(t // tm, f // tf),
        in_specs=[
            pl.BlockSpec(memory_space=pl.ANY),
            pl.BlockSpec((1, d), lambda i, j: (0, 0)),
            pl.BlockSpec((None, d, tf), lambda i, j: (layer, 0, j)),
            pl.BlockSpec((None, d, tf), lambda i, j: (layer, 0, j)),
            pl.BlockSpec((None, tf, d), lambda i, j: (layer, j, 0)),
        ],
        out_specs=pl.BlockSpec(memory_space=pl.ANY),
        out_shape=jax.ShapeDtypeStruct((t, d), F32),
        scratch_shapes=[pltpu.VMEM((tm, d), F32), pltpu.VMEM((tm, d), BF16),
                        pltpu.SemaphoreType.DMA((tm // rb,)), pltpu.SemaphoreType.DMA((tm // rb,))],
        compiler_params=_params("arbitrary", "arbitrary"),
        name="ffn",
    )(x2d, norm_g.reshape(1, d), w_gate, w_up, w_down)


def _kvq_kernel(x_ref, gkv_ref, gq_ref, w_ref, nk_ref, nq_ref, kq_ref, vt_ref, km_ref,
                hkv_scr, hq_scr, *, tm, tn, n_proj, q_scale):
    j = pl.program_id(1)
    heads = tn // HEAD_DIM
    blocks = tm // MOBA_BLOCK

    def norm(blk):
        for r in range(blk * MOBA_BLOCK, (blk + 1) * MOBA_BLOCK, NORM_ROWS):
            y = _rms_scale(x_ref[r:r + NORM_ROWS, :])
            hkv_scr[r:r + NORM_ROWS, :] = (y * gkv_ref[...]).astype(BF16)
            hq_scr[r:r + NORM_ROWS, :] = (y * gq_ref[...]).astype(BF16)

    def proj(h_scr, blk):
        rows = slice(blk * MOBA_BLOCK, (blk + 1) * MOBA_BLOCK)
        return rows, jnp.dot(h_scr[rows, :], w_ref[...].astype(BF16), preferred_element_type=F32)

    def k_trips(first_step):
        for blk in range(blocks):
            if first_step:
                norm(blk)
            rows, y = proj(hkv_scr, blk)
            for hh in range(heads):
                cols = slice(hh * HEAD_DIM, (hh + 1) * HEAD_DIM)
                kh = _rms_scale(y[:, cols]) * nk_ref[...]
                kq_ref[rows, cols] = kh.astype(kq_ref.dtype)
                km_ref[0, blk:blk + 1, cols] = jnp.mean(kh, axis=0, keepdims=True)

    @pl.when(j == 0)
    def _():
        k_trips(True)

    @pl.when(jnp.logical_and(j > 0, j < n_proj))
    def _():
        k_trips(False)

    @pl.when(jnp.logical_and(j >= n_proj, j < 2 * n_proj))
    def _():
        for blk in range(blocks):
            rows, y = proj(hkv_scr, blk)
            vt_ref[:, rows] = y.T.astype(vt_ref.dtype)

    @pl.when(j >= 2 * n_proj)
    def _():
        for blk in range(blocks):
            rows, y = proj(hq_scr, blk)
            for hh in range(heads):
                cols = slice(hh * HEAD_DIM, (hh + 1) * HEAD_DIM)
                qh = _rms_scale(y[:, cols]) * nq_ref[...] * q_scale
                kq_ref[rows, cols] = qh.astype(kq_ref.dtype)


def _kvq(x2d, kv_norm, q_in_norm, w_kvq, k_norm, q_norm):
    t, d = x2d.shape
    tm, tn = ROW_TILE, 1024
    n_proj = d // tn
    blocks = tm // MOBA_BLOCK
    kern = functools.partial(_kvq_kernel, tm=tm, tn=tn, n_proj=n_proj, q_scale=HEAD_DIM ** -0.5 * LOG2_E)

    def kq_map(i, j):
        return i, jnp.where(j < n_proj, j, jnp.maximum(j - n_proj, n_proj - 1))

    kq, vt, km = pl.pallas_call(
        kern,
        grid=(t // tm, 3 * n_proj),
        in_specs=[
            pl.BlockSpec((tm, d), lambda i, j: (i, 0)),
            pl.BlockSpec((1, d), lambda i, j: (0, 0)),
            pl.BlockSpec((1, d), lambda i, j: (0, 0)),
            pl.BlockSpec((d, tn), lambda i, j: (0, j)),
            pl.BlockSpec((1, HEAD_DIM), lambda i, j: (0, 0)),
            pl.BlockSpec((1, HEAD_DIM), lambda i, j: (0, 0)),
        ],
        out_specs=[
            pl.BlockSpec((tm, tn), kq_map),
            pl.BlockSpec((tn, tm), lambda i, j: (jnp.clip(j - n_proj, 0, n_proj - 1), i)),
            pl.BlockSpec((1, blocks, tn), lambda i, j: (i, 0, jnp.minimum(j, n_proj - 1))),
        ],
        out_shape=[jax.ShapeDtypeStruct((t, 2 * d), BF16),
                   jax.ShapeDtypeStruct((d, t), BF16),
                   jax.ShapeDtypeStruct((t // tm, blocks, d), F32)],
        scratch_shapes=[pltpu.VMEM((tm, d), BF16), pltpu.VMEM((tm, d), BF16)],
        compiler_params=_params("parallel", "arbitrary"),
        name="kvq",
    )(x2d, kv_norm.reshape(1, d), q_in_norm.reshape(1, d), w_kvq,
      k_norm.reshape(1, HEAD_DIM), q_norm.reshape(1, HEAD_DIM))
    return kq, vt, km.reshape(t // MOBA_BLOCK, d)


_NT = (((1,), (1,)), ((), ()))


def _attn_kernel(q_ref, k_ref, vt_ref, km_ref, o_ref, s_scr, p_scr, *, n_blocks):
    bs = MOBA_BLOCK
    sub = 8
    grp = bs // sub
    dh = HEAD_DIM
    n_heads = q_ref.shape[1] // dh
    blk_id = lax.broadcasted_iota(jnp.int32, (n_blocks, bs), 0)
    k_pos = lax.broadcasted_iota(jnp.int32, (bs, bs), 0)
    q_pos = lax.broadcasted_iota(jnp.int32, (bs, bs), 1)
    neg_inf = -jnp.inf
    causal = jnp.where(k_pos <= q_pos, 0.0, neg_inf).reshape(grp, sub, bs)

    n_slots = s_scr.shape[0]

    def scores(job):
        hd, i = jobs[job]
        cols = slice(hd * dh, (hd + 1) * dh)
        q = q_ref[i * bs:(i + 1) * bs, cols]
        nk = (i + 1) * bs
        s_scr[job % n_slots, 0:nk, :] = lax.dot_general(k_ref[0:nk, cols], q, _NT,
                                                        preferred_element_type=F32)
        if i == 0:
            return None
        km = km_ref[:, cols].astype(BF16)
        return lax.dot_general(km, q, _NT, preferred_element_type=F32)

    jobs = [(hd, i) for i in range(n_blocks - 1, -1, -1) for hd in range(n_heads)]
    gates = {job: scores(job) for job in range(min(ATTN_LOOKAHEAD, len(jobs)))}
    for job, (hd, i) in enumerate(jobs):
        cols = slice(hd * dh, (hd + 1) * dh)
        nk = (i + 1) * bs
        s_ref = s_scr.at[job % n_slots]
        p_ref = p_scr.at[job % 2]
        if job + ATTN_LOOKAHEAD < len(jobs):
            gates[job + ATTN_LOOKAHEAD] = scores(job + ATTN_LOOKAHEAD)
        gate = gates.pop(job)

        def piece(jb):
            return s_ref[jb * bs:(jb + 1) * bs, :].reshape(grp, sub, bs)

        biases = []
        if i > 0:
            rank = jnp.zeros((n_blocks, bs), F32)
            for m in range(i):
                gm = jnp.broadcast_to(gate[m:m + 1, :], (n_blocks, bs))
                beats = jnp.logical_or(gm > gate, jnp.logical_and(gm == gate, blk_id > m))
                rank = rank + jnp.where(beats, 1.0, 0.0)
            sel = jnp.logical_and(blk_id < i, rank < MOBA_TOPK)
            bias = jnp.where(sel, 0.0, neg_inf)
            biases = [jnp.broadcast_to(bias[jb:jb + 1, :], (sub, bs)) for jb in range(i)]
        m8 = jnp.max(piece(i) + causal, axis=0)
        for jb in range(i):
            m8 = jnp.maximum(m8, jnp.max(piece(jb), axis=0) + biases[jb])
        m_b = jnp.broadcast_to(jnp.max(m8, axis=0, keepdims=True), (sub, bs))
        l8 = jnp.zeros((sub, bs), F32)
        for jb in range(i + 1):
            if jb < i:
                p = jnp.exp2(piece(jb) - (m_b - biases[jb])[None])
            else:
                p = jnp.exp2(piece(jb) + causal - m_b[None])
            l8 = l8 + jnp.sum(p, axis=0)
            p_ref[jb * bs:(jb + 1) * bs, :] = p.reshape(bs, bs).astype(BF16)
        denom = jnp.sum(l8, axis=0, keepdims=True)
        o_t = jnp.dot(vt_ref[cols, 0:nk], p_ref[0:nk, :], preferred_element_type=F32)
        o_ref[i * bs:(i + 1) * bs, cols] = (o_t / denom).T.astype(o_ref.dtype)


def _attention(kq, vt, km, batch, seq, d):
    n_blocks = seq // MOBA_BLOCK
    w = ATTN_HEADS_PER_STEP * HEAD_DIM
    n_groups = d // w
    return pl.pallas_call(
        functools.partial(_attn_kernel, n_blocks=n_blocks),
        grid=(batch, n_groups),
        in_specs=[
            pl.BlockSpec((seq, w), lambda b, h: (b, n_groups + h)),
            pl.BlockSpec((seq, w), lambda b, h: (b, h)),
            pl.BlockSpec((w, seq), lambda b, h: (h, b)),
            pl.BlockSpec((n_blocks, w), lambda b, h: (b, h)),
        ],
        out_specs=pl.BlockSpec((seq, w), lambda b, h: (b, h)),
        out_shape=jax.ShapeDtypeStruct((batch * seq, d), BF16),
        scratch_shapes=[pltpu.VMEM((ATTN_LOOKAHEAD + 1, seq, MOBA_BLOCK), F32),
                        pltpu.VMEM((2, seq, MOBA_BLOCK), BF16)],
        compiler_params=_params("parallel", "arbitrary"),
        name="moba_attn",
    )(kq, kq, vt, km)


def kernel(x, a_norm, a_w_in, a_ln_g, a_ln_b, a_w_s, a_b_s, a_w_out, kv_norm, w_k, w_v, k_norm,
           b_norm, b_w_q, b_q_norm, b_w_o, ffn_norm, ffn_w_gate, ffn_w_up, ffn_w_down):
    batch, seq, d = x.shape
    assert seq % MOBA_BLOCK == 0 and seq // MOBA_BLOCK - 1 >= MOBA_TOPK
    n_a = a_norm.shape[0]
    depth = ffn_norm.shape[0]

    h = x.reshape(batch * seq, d)
    for layer in range(depth):
        if layer < n_a:
            i = layer
            ug = _sgu(h, a_norm[i], a_w_in, i, a_ln_g[i], a_ln_b[i], a_w_s[i], a_b_s[i])
            h = _mm_res(ug, a_w_out, i, h)
        else:
            i = layer - n_a
            assert depth - n_a == 1
            w_kvq = jnp.concatenate([w_k, w_v, b_w_q[i]], axis=1).astype(BF16)
            kq, vt, km = _kvq(h, kv_norm, b_norm[i], w_kvq, k_norm, b_q_norm[i])
            attn = _attention(kq, vt, km, batch, seq, d)
            h = _mm_res(attn, b_w_o, i, h)
        h = _ffn(h, ffn_norm[layer], ffn_w_gate, ffn_w_up, ffn_w_down, layer)
    return h.reshape(batch, seq, d)
```

```python
import functools

import jax
import jax.numpy as jnp
from jax import lax
from jax.experimental import pallas as pl
from jax.experimental.pallas import tpu as pltpu

F32 = jnp.float32
BF16 = jnp.bfloat16
EPS = 1e-6

SGU_CHUNK = 128
SGU_GROUPS = 16
HEAD_DIM = 128
MOBA_BLOCK = 256
MOBA_TOPK = 3

VMEM_LIMIT_BYTES = 56 * 1024 * 1024
ROW_TILE = 1024
ROW_BLOCK = 256
NORM_ROWS = 128
FFN_ROW_TILE = 2048
FFN_ROW_BLOCK = 512
MM_RES_ROWS = 512
LOG2_E = 1.4426950408889634
ATTN_HEADS_PER_STEP = 4
ATTN_LOOKAHEAD = 4


def _params(*sem):
    return pltpu.CompilerParams(dimension_semantics=sem, vmem_limit_bytes=VMEM_LIMIT_BYTES)


def _rows(n_rows, block, body, unroll=False):
    if unroll:
        for r in range(n_rows // block):
            body(r * block)
        return

    def step(r, carry):
        body(pl.multiple_of(r * block, block))
        return carry
    lax.fori_loop(0, n_rows // block, step, 0)


def _rms_scale(x):
    return x * lax.rsqrt(jnp.mean(x * x, axis=-1, keepdims=True) + EPS)


def _sgu_kernel(x_ref, g_ref, w_ref, lng_ref, lnb_ref, ws_ref, bs_ref, o_ref,
                h_scr, u_scr, v_scr, wm_scr, *, tm, tn, n_u, n_v):
    j = pl.program_id(1)
    last = n_u + n_v - 1
    d = n_v * tn
    c = SGU_CHUNK

    def mask_weights():
        t_idx = lax.broadcasted_iota(jnp.int32, (c, c), 0)
        s_idx = lax.broadcasted_iota(jnp.int32, (c, c), 1)
        for g in range(SGU_GROUPS):
            wm_scr[g] = jnp.where(s_idx <= t_idx, ws_ref[g], 0.0).astype(BF16)

    def norm(r0):
        for r in range(r0, r0 + ROW_BLOCK, NORM_ROWS):
            x = x_ref[r:r + NORM_ROWS, :]
            h_scr[r:r + NORM_ROWS, :] = (_rms_scale(x) * g_ref[...]).astype(BF16)

    def gelu_mm(r0):
        z = jnp.dot(h_scr[r0:r0 + ROW_BLOCK, :], w_ref[...].astype(BF16),
                    preferred_element_type=F32)
        return jax.nn.gelu(z, approximate=True)

    def gate_rows(r0):
        for r in range(r0, r0 + ROW_BLOCK, c):
            parts = [v_scr[t, r:r + c, :] for t in range(n_v)]
            mu = sum(jnp.sum(p, axis=-1, keepdims=True) for p in parts) / d
            cen = [p - mu for p in parts]
            var = sum(jnp.sum(x * x, axis=-1, keepdims=True) for x in cen) / d
            inv = lax.rsqrt(var + EPS)
            for t in range(n_v):
                cols = slice(t * tn, (t + 1) * tn)
                v_ln = (cen[t] * inv * lng_ref[:, cols] + lnb_ref[:, cols]).astype(BF16)
                for g in range(t * tn // c, (t + 1) * tn // c):
                    off = g * c - t * tn
                    mixed = jnp.dot(wm_scr[g], v_ln[:, off:off + c], preferred_element_type=F32)
                    gate = mixed + bs_ref[g]
                    u = u_scr[t, r:r + c, off:off + c].astype(F32)
                    o_ref[r:r + c, g * c:(g + 1) * c] = (u * gate).astype(o_ref.dtype)

    def trips(first_step, u_step, last_step):
        for r0 in range(0, tm, ROW_BLOCK):
            if first_step:
                norm(r0)
            z = gelu_mm(r0)
            if u_step:
                u_scr[j, r0:r0 + ROW_BLOCK, :] = z.astype(BF16)
            elif last_step:
                v_scr[n_v - 1, r0:r0 + ROW_BLOCK, :] = z
                gate_rows(r0)
            else:
                v_scr[j - n_u, r0:r0 + ROW_BLOCK, :] = z

    @pl.when(j == 0)
    def _():
        mask_weights()
        trips(True, True, False)

    @pl.when(jnp.logical_and(j > 0, j < n_u))
    def _():
        trips(False, True, False)

    @pl.when(jnp.logical_and(j >= n_u, j < last))
    def _():
        trips(False, False, False)

    @pl.when(j == last)
    def _():
        trips(False, False, True)


def _sgu(x2d, norm_g, w_in, layer, ln_g, ln_b, w_s, b_s):
    t, d = x2d.shape
    d_sgu = w_in.shape[2] // 2
    tm, tn = ROW_TILE, 512
    n_u = n_v = d_sgu // tn
    assert d_sgu == SGU_GROUPS * SGU_CHUNK and tn % SGU_CHUNK == 0
    kern = functools.partial(_sgu_kernel, tm=tm, tn=tn, n_u=n_u, n_v=n_v)
    return pl.pallas_call(
        kern,
        grid=(t // tm, n_u + n_v),
        in_specs=[
            pl.BlockSpec((tm, d), lambda i, j: (i, 0)),
            pl.BlockSpec((1, d), lambda i, j: (0, 0)),
            pl.BlockSpec((None, d, tn), lambda i, j: (layer, 0, j)),
            pl.BlockSpec((1, d_sgu), lambda i, j: (0, 0)),
            pl.BlockSpec((1, d_sgu), lambda i, j: (0, 0)),
            pl.BlockSpec(w_s.shape, lambda i, j: (0, 0, 0)),
            pl.BlockSpec((SGU_GROUPS, SGU_CHUNK, 1), lambda i, j: (0, 0, 0)),
        ],
        out_specs=pl.BlockSpec((tm, d_sgu), lambda i, j: (i, 0)),
        out_shape=jax.ShapeDtypeStruct((t, d_sgu), BF16),
        scratch_shapes=[pltpu.VMEM((tm, d), BF16), pltpu.VMEM((n_u, tm, tn), BF16),
                        pltpu.VMEM((n_v, tm, tn), F32),
                        pltpu.VMEM((SGU_GROUPS, SGU_CHUNK, SGU_CHUNK), BF16)],
        compiler_params=_params("parallel", "arbitrary"),
        name="sgu",
    )(x2d, norm_g.reshape(1, d), w_in, ln_g.reshape(1, d_sgu), ln_b.reshape(1, d_sgu),
      w_s, b_s.reshape(SGU_GROUPS, SGU_CHUNK, 1))


def _mm_res_kernel(a_ref, w_ref, x_ref, o_ref, *, tn):
    for c0 in range(0, o_ref.shape[1], tn):
        cols = slice(c0, c0 + tn)
        acc = jnp.dot(a_ref[...], w_ref[:, cols].astype(BF16), preferred_element_type=F32)
        o_ref[:, cols] = x_ref[:, cols] + acc


def _mm_res(a, w_stack, layer, x2d):
    t, k = a.shape
    n = w_stack.shape[2]
    tm = MM_RES_ROWS
    return pl.pallas_call(
        functools.partial(_mm_res_kernel, tn=512),
        grid=(t // tm,),
        in_specs=[
            pl.BlockSpec((tm, k), lambda i: (i, 0)),
            pl.BlockSpec((None, k, n), lambda i: (layer, 0, 0), pipeline_mode=pl.Buffered(1)),
            pl.BlockSpec((tm, n), lambda i: (i, 0)),
        ],
        out_specs=pl.BlockSpec((tm, n), lambda i: (i, 0)),
        out_shape=jax.ShapeDtypeStruct((t, n), F32),
        compiler_params=_params("parallel"),
        name="mm_res",
    )(a, w_stack, x2d)


def _ffn_kernel(x_hbm, g_ref, wg_ref, wu_ref, wd_ref, o_hbm, acc, h_scr, in_sem, out_sem,
                *, tm, rb):
    i = pl.program_id(0)
    j = pl.program_id(1)
    n_i = pl.num_programs(0)
    n_j = pl.num_programs(1)
    n_blk = tm // rb

    def in_copy(tile, c):
        return pltpu.make_async_copy(x_hbm.at[pl.ds(tile * tm + c * rb, rb), :],
                                     acc.at[pl.ds(c * rb, rb), :], in_sem.at[c])

    def out_copy(tile, c):
        return pltpu.make_async_copy(acc.at[pl.ds(c * rb, rb), :],
                                     o_hbm.at[pl.ds(tile * tm + c * rb, rb), :], out_sem.at[c])

    @pl.when(j == 0)
    def _():
        @pl.when(i == 0)
        def _():
            for c in range(n_blk):
                in_copy(0, c).start()

        @pl.when(i > 0)
        def _():
            out_copy(i - 1, n_blk - 1).wait()
            in_copy(i, n_blk - 1).start()

        for c in range(n_blk):
            in_copy(i, c).wait()

    def norm(r):
        for r0 in range(r * rb, (r + 1) * rb, NORM_ROWS):
            rows = pl.ds(r0, NORM_ROWS)
            h_scr[rows, :] = (_rms_scale(acc[rows, :]) * g_ref[...]).astype(BF16)

    def trips(write_back, first_step=False):
        for r in range(n_blk):
            rows = pl.ds(r * rb, rb)
            if first_step:
                norm(r)
            hb = h_scr[rows, :]
            gate = jnp.dot(hb, wg_ref[...].astype(BF16), preferred_element_type=F32)
            up = jnp.dot(hb, wu_ref[...].astype(BF16), preferred_element_type=F32)
            act = (jax.nn.silu(gate) * up).astype(BF16)
            acc[rows, :] += jnp.dot(act, wd_ref[...].astype(BF16), preferred_element_type=F32)
            if write_back:
                out_copy(i, r).start()
                if r > 0:
                    @pl.when(i < n_i - 1)
                    def _(r=r):
                        out_copy(i, r - 1).wait()
                        in_copy(i + 1, r - 1).start()

    @pl.when(j == 0)
    def _():
        trips(False, first_step=True)

    @pl.when(jnp.logical_and(j > 0, j < n_j - 1))
    def _():
        trips(False)

    @pl.when(j == n_j - 1)
    def _():
        trips(True)

    @pl.when(jnp.logical_and(i == n_i - 1, j == n_j - 1))
    def _():
        for c in range(n_blk):
            out_copy(i, c).wait()


def _ffn(x2d, norm_g, w_gate, w_up, w_down, layer):
    t, d = x2d.shape
    f = w_gate.shape[2]
    tm, tf, rb = FFN_ROW_TILE, 512, FFN_ROW_BLOCK
    return pl.pallas_call(
        functools.partial(_ffn_kernel, tm=tm, rb=rb),
        grid=(t // tm, f // tf),
        in_specs=[
            pl.BlockSpec(memory_space=pl.ANY),
            pl.BlockSpec((1, d), lambda i, j: (0, 0)),
            pl.BlockSpec((None, d, tf), lambda i, j: (layer, 0, j)),
            pl.BlockSpec((None, d, tf), lambda i, j: (layer, 0, j)),
            pl.BlockSpec((None, tf, d), lambda i, j: (layer, j, 0)),
        ],
        out_specs=pl.BlockSpec(memory_space=pl.ANY),
        out_shape=jax.ShapeDtypeStruct((t, d), F32),
        scratch_shapes=[pltpu.VMEM((tm, d), F32), pltpu.VMEM((tm, d), BF16),
                        pltpu.SemaphoreType.DMA((tm // rb,)), pltpu.SemaphoreType.DMA((tm // rb,))],
        compiler_params=_params("arbitrary", "arbitrary"),
        name="ffn",
    )(x2d, norm_g.reshape(1, d), w_gate, w_up, w_down)


def _kvq_kernel(x_ref, gkv_ref, gq_ref, w_ref, nk_ref, nq_ref, kq_ref, vt_ref, km_ref,
                hkv_scr, hq_scr, *, tm, tn, n_proj, q_scale):
    j = pl.program_id(1)
    heads = tn // HEAD_DIM
    blocks = tm // MOBA_BLOCK

    def norm(blk):
        for r in range(blk * MOBA_BLOCK, (blk + 1) * MOBA_BLOCK, NORM_ROWS):
            y = _rms_scale(x_ref[r:r + NORM_ROWS, :])
            hkv_scr[r:r + NORM_ROWS, :] = (y * gkv_ref[...]).astype(BF16)
            hq_scr[r:r + NORM_ROWS, :] = (y * gq_ref[...]).astype(BF16)

    def proj(h_scr, blk):
        rows = slice(blk * MOBA_BLOCK, (blk + 1) * MOBA_BLOCK)
        return rows, jnp.dot(h_scr[rows, :], w_ref[...].astype(BF16), preferred_element_type=F32)

    def k_trips(first_step):
        for blk in range(blocks):
            if first_step:
                norm(blk)
            rows, y = proj(hkv_scr, blk)
            for hh in range(heads):
                cols = slice(hh * HEAD_DIM, (hh + 1) * HEAD_DIM)
                kh = _rms_scale(y[:, cols]) * nk_ref[...]
                kq_ref[rows, cols] = kh.astype(kq_ref.dtype)
                km_ref[0, blk:blk + 1, cols] = jnp.mean(kh, axis=0, keepdims=True)

    @pl.when(j == 0)
    def _():
        k_trips(True)

    @pl.when(jnp.logical_and(j > 0, j < n_proj))
    def _():
        k_trips(False)

    @pl.when(jnp.logical_and(j >= n_proj, j < 2 * n_proj))
    def _():
        for blk in range(blocks):
            rows, y = proj(hkv_scr, blk)
            vt_ref[:, rows] = y.T.astype(vt_ref.dtype)

    @pl.when(j >= 2 * n_proj)
    def _():
        for blk in range(blocks):
            rows, y = proj(hq_scr, blk)
            for hh in range(heads):
                cols = slice(hh * HEAD_DIM, (hh + 1) * HEAD_DIM)
                qh = _rms_scale(y[:, cols]) * nq_ref[...] * q_scale
                kq_ref[rows, cols] = qh.astype(kq_ref.dtype)


def _kvq(x2d, kv_norm, q_in_norm, w_kvq, k_norm, q_norm):
    t, d = x2d.shape
    tm, tn = ROW_TILE, 1024
    n_proj = d // tn
    blocks = tm // MOBA_BLOCK
    kern = functools.partial(_kvq_kernel, tm=tm, tn=tn, n_proj=n_proj, q_scale=HEAD_DIM ** -0.5 * LOG2_E)

    def kq_map(i, j):
        return i, jnp.where(j < n_proj, j, jnp.maximum(j - n_proj, n_proj - 1))

    kq, vt, km = pl.pallas_call(
        kern,
        grid=(t // tm, 3 * n_proj),
        in_specs=[
            pl.BlockSpec((tm, d), lambda i, j: (i, 0)),
            pl.BlockSpec((1, d), lambda i, j: (0, 0)),
            pl.BlockSpec((1, d), lambda i, j: (0, 0)),
            pl.BlockSpec((d, tn), lambda i, j: (0, j)),
            pl.BlockSpec((1, HEAD_DIM), lambda i, j: (0, 0)),
            pl.BlockSpec((1, HEAD_DIM), lambda i, j: (0, 0)),
        ],
        out_specs=[
            pl.BlockSpec((tm, tn), kq_map),
            pl.BlockSpec((tn, tm), lambda i, j: (jnp.clip(j - n_proj, 0, n_proj - 1), i)),
            pl.BlockSpec((1, blocks, tn), lambda i, j: (i, 0, jnp.minimum(j, n_proj - 1))),
        ],
        out_shape=[jax.ShapeDtypeStruct((t, 2 * d), BF16),
                   jax.ShapeDtypeStruct((d, t), BF16),
                   jax.ShapeDtypeStruct((t // tm, blocks, d), F32)],
        scratch_shapes=[pltpu.VMEM((tm, d), BF16), pltpu.VMEM((tm, d), BF16)],
        compiler_params=_params("parallel", "arbitrary"),
        name="kvq",
    )(x2d, kv_norm.reshape(1, d), q_in_norm.reshape(1, d), w_kvq,
      k_norm.reshape(1, HEAD_DIM), q_norm.reshape(1, HEAD_DIM))
    return kq, vt, km.reshape(t // MOBA_BLOCK, d)


_NT = (((1,), (1,)), ((), ()))


def _attn_kernel(q_ref, k_ref, vt_ref, km_ref, o_ref, s_scr, p_scr, *, n_blocks):
    bs = MOBA_BLOCK
    sub = 8
    grp = bs // sub
    dh = HEAD_DIM
    n_heads = q_ref.shape[1] // dh
    blk_id = lax.broadcasted_iota(jnp.int32, (n_blocks, bs), 0)
    k_pos = lax.broadcasted_iota(jnp.int32, (bs, bs), 0)
    q_pos = lax.broadcasted_iota(jnp.int32, (bs, bs), 1)
    neg_inf = -jnp.inf
    causal = jnp.where(k_pos <= q_pos, 0.0, neg_inf).reshape(grp, sub, bs)

    n_slots = s_scr.shape[0]

    def scores(job):
        hd, i = jobs[job]
        cols = slice(hd * dh, (hd + 1) * dh)
        q = q_ref[i * bs:(i + 1) * bs, cols]
        nk = (i + 1) * bs
        s_scr[job % n_slots, 0:nk, :] = lax.dot_general(k_ref[0:nk, cols], q, _NT,
                                                        preferred_element_type=F32)
        if i == 0:
            return None
        km = km_ref[:, cols].astype(BF16)
        return lax.dot_general(km, q, _NT, preferred_element_type=F32)

    jobs = [(hd, i) for i in range(n_blocks - 1, -1, -1) for hd in range(n_heads)]
    gates = {job: scores(job) for job in range(min(ATTN_LOOKAHEAD, len(jobs)))}
    for job, (hd, i) in enumerate(jobs):
        cols = slice(hd * dh, (hd + 1) * dh)
        nk = (i + 1) * bs
        s_ref = s_scr.at[job % n_slots]
        p_ref = p_scr.at[job % 2]
        if job + ATTN_LOOKAHEAD < len(jobs):
            gates[job + ATTN_LOOKAHEAD] = scores(job + ATTN_LOOKAHEAD)
        gate = gates.pop(job)

        def piece(jb):
            return s_ref[jb * bs:(jb + 1) * bs, :].reshape(grp, sub, bs)

        biases = []
        if i > 0:
            rank = jnp.zeros((n_blocks, bs), F32)
            for m in range(i):
                gm = jnp.broadcast_to(gate[m:m + 1, :], (n_blocks, bs))
                beats = jnp.logical_or(gm > gate, jnp.logical_and(gm == gate, blk_id > m))
                rank = rank + jnp.where(beats, 1.0, 0.0)
            sel = jnp.logical_and(blk_id < i, rank < MOBA_TOPK)
            bias = jnp.where(sel, 0.0, neg_inf)
            biases = [jnp.broadcast_to(bias[jb:jb + 1, :], (sub, bs)) for jb in range(i)]
        m8 = jnp.max(piece(i) + causal, axis=0)
        for jb in range(i):
            m8 = jnp.maximum(m8, jnp.max(piece(jb), axis=0) + biases[jb])
        m_b = jnp.broadcast_to(jnp.max(m8, axis=0, keepdims=True), (sub, bs))
        l8 = jnp.zeros((sub, bs), F32)
        for jb in range(i + 1):
            if jb < i:
                p = jnp.exp2(piece(jb) - (m_b - biases[jb])[None])
            else:
                p = jnp.exp2(piece(jb) + causal - m_b[None])
            l8 = l8 + jnp.sum(p, axis=0)
            p_ref[jb * bs:(jb + 1) * bs, :] = p.reshape(bs, bs).astype(BF16)
        denom = jnp.sum(l8, axis=0, keepdims=True)
        o_t = jnp.dot(vt_ref[cols, 0:nk], p_ref[0:nk, :], preferred_element_type=F32)
        o_ref[i * bs:(i + 1) * bs, cols] = (o_t / denom).T.astype(o_ref.dtype)


def _attention(kq, vt, km, batch, seq, d):
    n_blocks = seq // MOBA_BLOCK
    w = ATTN_HEADS_PER_STEP * HEAD_DIM
    n_groups = d // w
    return pl.pallas_call(
        functools.partial(_attn_kernel, n_blocks=n_blocks),
        grid=(batch, n_groups),
        in_specs=[
            pl.BlockSpec((seq, w), lambda b, h: (b, n_groups + h)),
            pl.BlockSpec((seq, w), lambda b, h: (b, h)),
            pl.BlockSpec((w, seq), lambda b, h: (h, b)),
            pl.BlockSpec((n_blocks, w), lambda b, h: (b, h)),
        ],
        out_specs=pl.BlockSpec((seq, w), lambda b, h: (b, h)),
        out_shape=jax.ShapeDtypeStruct((batch * seq, d), BF16),
        scratch_shapes=[pltpu.VMEM((ATTN_LOOKAHEAD + 1, seq, MOBA_BLOCK), F32),
                        pltpu.VMEM((2, seq, MOBA_BLOCK), BF16)],
        compiler_params=_params("parallel", "arbitrary"),
        name="moba_attn",
    )(kq, kq, vt, km)


def kernel(x, a_norm, a_w_in, a_ln_g, a_ln_b, a_w_s, a_b_s, a_w_out, kv_norm, w_k, w_v, k_norm,
           b_norm, b_w_q, b_q_norm, b_w_o, ffn_norm, ffn_w_gate, ffn_w_up, ffn_w_down):
    batch, seq, d = x.shape
    assert seq % MOBA_BLOCK == 0 and seq // MOBA_BLOCK - 1 >= MOBA_TOPK
    n_a = a_norm.shape[0]
    depth = ffn_norm.shape[0]

    h = x.reshape(batch * seq, d)
    for layer in range(depth):
        if layer < n_a:
            i = layer
            ug = _sgu(h, a_norm[i], a_w_in, i, a_ln_g[i], a_ln_b[i], a_w_s[i], a_b_s[i])
            h = _mm_res(ug, a_w_out, i, h)
        else:
            i = layer - n_a
            assert depth - n_a == 1
            w_kvq = jnp.concatenate([w_k, w_v, b_w_q[i]], axis=1).astype(BF16)
            kq, vt, km = _kvq(h, kv_norm, b_norm[i], w_kvq, k_norm, b_q_norm[i])
            attn = _attention(kq, vt, km, batch, seq, d)
            h = _mm_res(attn, b_w_o, i, h)
        h = _ffn(h, ffn_norm[layer], ffn_w_gate, ffn_w_up, ffn_w_down, layer)
    return h.reshape(batch, seq, d)
```

```python
import functools

import jax
import jax.numpy as jnp
from jax import lax
from jax.experimental import pallas as pl
from jax.experimental.pallas import tpu as pltpu

F32 = jnp.float32
BF16 = jnp.bfloat16
EPS = 1e-6

SGU_CHUNK = 128
SGU_GROUPS = 16
HEAD_DIM = 128
MOBA_BLOCK = 256
MOBA_TOPK = 3

VMEM_LIMIT_BYTES = 56 * 1024 * 1024
ROW_TILE = 1024
ROW_BLOCK = 256
NORM_ROWS = 128
FFN_ROW_TILE = 2048
FFN_ROW_BLOCK = 512
MM_RES_ROWS = 512
LOG2_E = 1.4426950408889634
ATTN_HEADS_PER_STEP = 4
ATTN_LOOKAHEAD = 3


def _params(*sem):
    return pltpu.CompilerParams(dimension_semantics=sem, vmem_limit_bytes=VMEM_LIMIT_BYTES)


def _rows(n_rows, block, body):
    def step(r, carry):
        body(pl.multiple_of(r * block, block))
        return carry
    lax.fori_loop(0, n_rows // block, step, 0)


def _rms_scale(x):
    return x * lax.rsqrt(jnp.mean(x * x, axis=-1, keepdims=True) + EPS)


def _sgu_kernel(x_ref, g_ref, w_ref, lng_ref, lnb_ref, ws_ref, bs_ref, o_ref,
                h_scr, u_scr, v_scr, wm_scr, *, tm, tn, n_u, n_v):
    j = pl.program_id(1)
    last = n_u + n_v - 1
    d = n_v * tn
    c = SGU_CHUNK

    def mask_weights():
        t_idx = lax.broadcasted_iota(jnp.int32, (c, c), 0)
        s_idx = lax.broadcasted_iota(jnp.int32, (c, c), 1)
        for g in range(SGU_GROUPS):
            wm_scr[g] = jnp.where(s_idx <= t_idx, ws_ref[g], 0.0).astype(BF16)

    def norm(r0):
        for r in range(r0, r0 + ROW_BLOCK, NORM_ROWS):
            x = x_ref[r:r + NORM_ROWS, :]
            h_scr[r:r + NORM_ROWS, :] = (_rms_scale(x) * g_ref[...]).astype(BF16)

    def gelu_mm(r0):
        z = jnp.dot(h_scr[r0:r0 + ROW_BLOCK, :], w_ref[...].astype(BF16),
                    preferred_element_type=F32)
        return jax.nn.gelu(z, approximate=True)

    def gate_rows(r0):
        for r in range(r0, r0 + ROW_BLOCK, c):
            parts = [v_scr[t, r:r + c, :] for t in range(n_v)]
            mu = sum(jnp.sum(p, axis=-1, keepdims=True) for p in parts) / d
            cen = [p - mu for p in parts]
            var = sum(jnp.sum(x * x, axis=-1, keepdims=True) for x in cen) / d
            inv = lax.rsqrt(var + EPS)
            for t in range(n_v):
                cols = slice(t * tn, (t + 1) * tn)
                v_ln = (cen[t] * inv * lng_ref[:, cols] + lnb_ref[:, cols]).astype(BF16)
                for g in range(t * tn // c, (t + 1) * tn // c):
                    off = g * c - t * tn
                    mixed = jnp.dot(wm_scr[g], v_ln[:, off:off + c], preferred_element_type=F32)
                    gate = mixed + bs_ref[g]
                    u = u_scr[t, r:r + c, off:off + c].astype(F32)
                    o_ref[r:r + c, g * c:(g + 1) * c] = (u * gate).astype(o_ref.dtype)

    def trips(first_step, u_step, last_step):
        for r0 in range(0, tm, ROW_BLOCK):
            if first_step:
                norm(r0)
            z = gelu_mm(r0)
            if u_step:
                u_scr[j, r0:r0 + ROW_BLOCK, :] = z.astype(BF16)
            elif last_step:
                v_scr[n_v - 1, r0:r0 + ROW_BLOCK, :] = z
                gate_rows(r0)
            else:
                v_scr[j - n_u, r0:r0 + ROW_BLOCK, :] = z

    @pl.when(j == 0)
    def _():
        mask_weights()
        trips(True, True, False)

    @pl.when(jnp.logical_and(j > 0, j < n_u))
    def _():
        trips(False, True, False)

    @pl.when(jnp.logical_and(j >= n_u, j < last))
    def _():
        trips(False, False, False)

    @pl.when(j == last)
    def _():
        trips(False, False, True)


def _sgu(x2d, norm_g, w_in, layer, ln_g, ln_b, w_s, b_s):
    t, d = x2d.shape
    d_sgu = w_in.shape[2] // 2
    tm, tn = ROW_TILE, 512
    n_u = n_v = d_sgu // tn
    assert d_sgu == SGU_GROUPS * SGU_CHUNK and tn % SGU_CHUNK == 0
    kern = functools.partial(_sgu_kernel, tm=tm, tn=tn, n_u=n_u, n_v=n_v)
    return pl.pallas_call(
        kern,
        grid=(t // tm, n_u + n_v),
        in_specs=[
            pl.BlockSpec((tm, d), lambda i, j: (i, 0)),
            pl.BlockSpec((1, d), lambda i, j: (0, 0)),
            pl.BlockSpec((None, d, tn), lambda i, j: (layer, 0, j)),
            pl.BlockSpec((1, d_sgu), lambda i, j: (0, 0)),
            pl.BlockSpec((1, d_sgu), lambda i, j: (0, 0)),
            pl.BlockSpec(w_s.shape, lambda i, j: (0, 0, 0)),
            pl.BlockSpec((SGU_GROUPS, SGU_CHUNK, 1), lambda i, j: (0, 0, 0)),
        ],
        out_specs=pl.BlockSpec((tm, d_sgu), lambda i, j: (i, 0)),
        out_shape=jax.ShapeDtypeStruct((t, d_sgu), BF16),
        scratch_shapes=[pltpu.VMEM((tm, d), BF16), pltpu.VMEM((n_u, tm, tn), BF16),
                        pltpu.VMEM((n_v, tm, tn), F32),
                        pltpu.VMEM((SGU_GROUPS, SGU_CHUNK, SGU_CHUNK), BF16)],
        compiler_params=_params("parallel", "arbitrary"),
        name="sgu",
    )(x2d, norm_g.reshape(1, d), w_in, ln_g.reshape(1, d_sgu), ln_b.reshape(1, d_sgu),
      w_s, b_s.reshape(SGU_GROUPS, SGU_CHUNK, 1))


def _mm_res_kernel(a_ref, w_ref, x_ref, o_ref, *, tn):
    for c0 in range(0, o_ref.shape[1], tn):
        cols = slice(c0, c0 + tn)
        acc = jnp.dot(a_ref[...], w_ref[:, cols].astype(BF16), preferred_element_type=F32)
        o_ref[:, cols] = x_ref[:, cols] + acc


def _mm_res(a, w_stack, layer, x2d):
    t, k = a.shape
    n = w_stack.shape[2]
    tm = MM_RES_ROWS
    return pl.pallas_call(
        functools.partial(_mm_res_kernel, tn=512),
        grid=(t // tm,),
        in_specs=[
            pl.BlockSpec((tm, k), lambda i: (i, 0)),
            pl.BlockSpec((None, k, n), lambda i: (layer, 0, 0), pipeline_mode=pl.Buffered(1)),
            pl.BlockSpec((tm, n), lambda i: (i, 0)),
        ],
        out_specs=pl.BlockSpec((tm, n), lambda i: (i, 0)),
        out_shape=jax.ShapeDtypeStruct((t, n), F32),
        compiler_params=_params("parallel"),
        name="mm_res",
    )(a, w_stack, x2d)


def _ffn_kernel(x_hbm, g_ref, wg_ref, wu_ref, wd_ref, o_hbm, acc, h_scr, in_sem, out_sem,
                *, tm, rb):
    i = pl.program_id(0)
    j = pl.program_id(1)
    n_i = pl.num_programs(0)
    n_j = pl.num_programs(1)
    n_blk = tm // rb

    def in_copy(tile, c):
        return pltpu.make_async_copy(x_hbm.at[pl.ds(tile * tm + c * rb, rb), :],
                                     acc.at[pl.ds(c * rb, rb), :], in_sem.at[c])

    def out_copy(tile, c):
        return pltpu.make_async_copy(acc.at[pl.ds(c * rb, rb), :],
                                     o_hbm.at[pl.ds(tile * tm + c * rb, rb), :], out_sem.at[c])

    @pl.when(j == 0)
    def _():
        @pl.when(i == 0)
        def _():
            for c in range(n_blk):
                in_copy(0, c).start()

        @pl.when(i > 0)
        def _():
            out_copy(i - 1, n_blk - 1).wait()
            in_copy(i, n_blk - 1).start()

        for c in range(n_blk):
            in_copy(i, c).wait()

            def norm(r0):
                rows = pl.ds(pl.multiple_of(c * rb + r0, NORM_ROWS), NORM_ROWS)
                h_scr[rows, :] = (_rms_scale(acc[rows, :]) * g_ref[...]).astype(BF16)
            _rows(rb, NORM_ROWS, norm)

    def trips(write_back):
        for r in range(n_blk):
            rows = pl.ds(r * rb, rb)
            hb = h_scr[rows, :]
            gate = jnp.dot(hb, wg_ref[...].astype(BF16), preferred_element_type=F32)
            up = jnp.dot(hb, wu_ref[...].astype(BF16), preferred_element_type=F32)
            act = (jax.nn.silu(gate) * up).astype(BF16)
            acc[rows, :] += jnp.dot(act, wd_ref[...].astype(BF16), preferred_element_type=F32)
            if write_back:
                out_copy(i, r).start()
                if r > 0:
                    @pl.when(i < n_i - 1)
                    def _(r=r):
                        out_copy(i, r - 1).wait()
                        in_copy(i + 1, r - 1).start()

    @pl.when(j < n_j - 1)
    def _():
        trips(False)

    @pl.when(j == n_j - 1)
    def _():
        trips(True)

    @pl.when(jnp.logical_and(i == n_i - 1, j == n_j - 1))
    def _():
        for c in range(n_blk):
            out_copy(i, c).wait()


def _ffn(x2d, norm_g, w_gate, w_up, w_down, layer):
    t, d = x2d.shape
    f = w_gate.shape[2]
    tm, tf, rb = FFN_ROW_TILE, 512, FFN_ROW_BLOCK
    return pl.pallas_call(
        functools.partial(_ffn_kernel, tm=tm, rb=rb),
        grid=(t // tm, f // tf),
        in_specs=[
            pl.BlockSpec(memory_space=pl.ANY),
            pl.BlockSpec((1, d), lambda i, j: (0, 0)),
            pl.BlockSpec((None, d, tf), lambda i, j: (layer, 0, j)),
            pl.BlockSpec((None, d, tf), lambda i, j: (layer, 0, j)),
            pl.BlockSpec((None, tf, d), lambda i, j: (layer, j, 0)),
        ],
        out_specs=pl.BlockSpec(memory_space=pl.ANY),
        out_shape=jax.ShapeDtypeStruct((t, d), F32),
        scratch_shapes=[pltpu.VMEM((tm, d), F32), pltpu.VMEM((tm, d), BF16),
                        pltpu.SemaphoreType.DMA((tm // rb,)), pltpu.SemaphoreType.DMA((tm // rb,))],
        compiler_params=_params("arbitrary", "arbitrary"),
        name="ffn",
    )(x2d, norm_g.reshape(1, d), w_gate, w_up, w_down)


def _kvq_kernel(x_ref, gkv_ref, gq_ref, w_ref, nk_ref, nq_ref, kq_ref, vt_ref, km_ref,
                hkv_scr, hq_scr, *, tm, tn, n_proj, q_scale):
    j = pl.program_id(1)
    heads = tn // HEAD_DIM
    blocks = tm // MOBA_BLOCK

    def norm(blk):
        for r in range(blk * MOBA_BLOCK, (blk + 1) * MOBA_BLOCK, NORM_ROWS):
            y = _rms_scale(x_ref[r:r + NORM_ROWS, :])
            hkv_scr[r:r + NORM_ROWS, :] = (y * gkv_ref[...]).astype(BF16)
            hq_scr[r:r + NORM_ROWS, :] = (y * gq_ref[...]).astype(BF16)

    def proj(h_scr, blk):
        rows = slice(blk * MOBA_BLOCK, (blk + 1) * MOBA_BLOCK)
        return rows, jnp.dot(h_scr[rows, :], w_ref[...], preferred_element_type=F32)

    def k_trips(first_step):
        for blk in range(blocks):
            if first_step:
                norm(blk)
            rows, y = proj(hkv_scr, blk)
            for hh in range(heads):
                cols = slice(hh * HEAD_DIM, (hh + 1) * HEAD_DIM)
                kh = _rms_scale(y[:, cols]) * nk_ref[...]
                kq_ref[rows, cols] = kh.astype(kq_ref.dtype)
                km_ref[0, blk:blk + 1, cols] = jnp.mean(kh, axis=0, keepdims=True)

    @pl.when(j == 0)
    def _():
        k_trips(True)

    @pl.when(jnp.logical_and(j > 0, j < n_proj))
    def _():
        k_trips(False)

    @pl.when(jnp.logical_and(j >= n_proj, j < 2 * n_proj))
    def _():
        for blk in range(blocks):
            rows, y = proj(hkv_scr, blk)
            vt_ref[:, rows] = y.T.astype(vt_ref.dtype)

    @pl.when(j >= 2 * n_proj)
    def _():
        for blk in range(blocks):
            rows, y = proj(hq_scr, blk)
            for hh in range(heads):
                cols = slice(hh * HEAD_DIM, (hh + 1) * HEAD_DIM)
                qh = _rms_scale(y[:, cols]) * nq_ref[...] * q_scale
                kq_ref[rows, cols] = qh.astype(kq_ref.dtype)


def _kvq(x2d, kv_norm, q_in_norm, w_kvq, k_norm, q_norm):
    t, d = x2d.shape
    tm, tn = ROW_TILE, 1024
    n_proj = d // tn
    blocks = tm // MOBA_BLOCK
    kern = functools.partial(_kvq_kernel, tm=tm, tn=tn, n_proj=n_proj, q_scale=HEAD_DIM ** -0.5 * LOG2_E)

    def kq_map(i, j):
        return i, jnp.where(j < n_proj, j, jnp.maximum(j - n_proj, n_proj - 1))

    kq, vt, km = pl.pallas_call(
        kern,
        grid=(t // tm, 3 * n_proj),
        in_specs=[
            pl.BlockSpec((tm, d), lambda i, j: (i, 0)),
            pl.BlockSpec((1, d), lambda i, j: (0, 0)),
            pl.BlockSpec((1, d), lambda i, j: (0, 0)),
            pl.BlockSpec((None, d, tn), lambda i, j: (j // n_proj, 0, j % n_proj)),
            pl.BlockSpec((1, HEAD_DIM), lambda i, j: (0, 0)),
            pl.BlockSpec((1, HEAD_DIM), lambda i, j: (0, 0)),
        ],
        out_specs=[
            pl.BlockSpec((tm, tn), kq_map),
            pl.BlockSpec((tn, tm), lambda i, j: (jnp.clip(j - n_proj, 0, n_proj - 1), i)),
            pl.BlockSpec((1, blocks, tn), lambda i, j: (i, 0, jnp.minimum(j, n_proj - 1))),
        ],
        out_shape=[jax.ShapeDtypeStruct((t, 2 * d), BF16),
                   jax.ShapeDtypeStruct((d, t), BF16),
                   jax.ShapeDtypeStruct((t // tm, blocks, d), F32)],
        scratch_shapes=[pltpu.VMEM((tm, d), BF16), pltpu.VMEM((tm, d), BF16)],
        compiler_params=_params("parallel", "arbitrary"),
        name="kvq",
    )(x2d, kv_norm.reshape(1, d), q_in_norm.reshape(1, d), w_kvq,
      k_norm.reshape(1, HEAD_DIM), q_norm.reshape(1, HEAD_DIM))
    return kq, vt, km.reshape(t // MOBA_BLOCK, d)


_NT = (((1,), (1,)), ((), ()))


def _attn_kernel(q_ref, k_ref, vt_ref, km_ref, o_ref, s_scr, p_scr, *, n_blocks):
    bs = MOBA_BLOCK
    sub = 8
    grp = bs // sub
    dh = HEAD_DIM
    n_heads = q_ref.shape[1] // dh
    blk_id = lax.broadcasted_iota(jnp.int32, (n_blocks, bs), 0)
    k_pos = lax.broadcasted_iota(jnp.int32, (bs, bs), 0)
    q_pos = lax.broadcasted_iota(jnp.int32, (bs, bs), 1)
    neg_inf = -jnp.inf
    causal = jnp.where(k_pos <= q_pos, 0.0, neg_inf).reshape(grp, sub, bs)

    n_slots = s_scr.shape[0]

    def scores(job):
        hd, i = jobs[job]
        cols = slice(hd * dh, (hd + 1) * dh)
        q = q_ref[i * bs:(i + 1) * bs, cols]
        nk = (i + 1) * bs
        s_scr[job % n_slots, 0:nk, :] = lax.dot_general(k_ref[0:nk, cols], q, _NT,
                                                        preferred_element_type=F32)
        if i == 0:
            return None
        km = km_ref[:, cols].astype(BF16)
        return lax.dot_general(km, q, _NT, preferred_element_type=F32)

    jobs = [(hd, i) for i in range(n_blocks - 1, -1, -1) for hd in range(n_heads)]
    gates = {job: scores(job) for job in range(min(ATTN_LOOKAHEAD, len(jobs)))}
    for job, (hd, i) in enumerate(jobs):
        cols = slice(hd * dh, (hd + 1) * dh)
        nk = (i + 1) * bs
        s_ref = s_scr.at[job % n_slots]
        p_ref = p_scr.at[job % 2]
        if job + ATTN_LOOKAHEAD < len(jobs):
            gates[job + ATTN_LOOKAHEAD] = scores(job + ATTN_LOOKAHEAD)
        gate = gates.pop(job)

        def piece(jb):
            return s_ref[jb * bs:(jb + 1) * bs, :].reshape(grp, sub, bs)

        biases = []
        if i > 0:
            rank = jnp.zeros((n_blocks, bs), F32)
            for m in range(i):
                gm = jnp.broadcast_to(gate[m:m + 1, :], (n_blocks, bs))
                beats = jnp.logical_or(gm > gate, jnp.logical_and(gm == gate, blk_id > m))
                rank = rank + jnp.where(beats, 1.0, 0.0)
            sel = jnp.logical_and(blk_id < i, rank < MOBA_TOPK)
            bias = jnp.where(sel, 0.0, neg_inf)
            biases = [jnp.broadcast_to(bias[jb:jb + 1, :], (sub, bs)) for jb in range(i)]
        m8 = jnp.max(piece(i) + causal, axis=0)
        for jb in range(i):
            m8 = jnp.maximum(m8, jnp.max(piece(jb), axis=0) + biases[jb])
        m_b = jnp.broadcast_to(jnp.max(m8, axis=0, keepdims=True), (sub, bs))
        l8 = jnp.zeros((sub, bs), F32)
        for jb in range(i + 1):
            if jb < i:
                p = jnp.exp2(piece(jb) - (m_b - biases[jb])[None])
            else:
                p = jnp.exp2(piece(jb) + causal - m_b[None])
            l8 = l8 + jnp.sum(p, axis=0)
            p_ref[jb * bs:(jb + 1) * bs, :] = p.reshape(bs, bs).astype(BF16)
        denom = jnp.sum(l8, axis=0, keepdims=True)
        o_t = jnp.dot(vt_ref[cols, 0:nk], p_ref[0:nk, :], preferred_element_type=F32)
        o_ref[i * bs:(i + 1) * bs, cols] = (o_t / denom).T.astype(o_ref.dtype)


def _attention(kq, vt, km, batch, seq, d):
    n_blocks = seq // MOBA_BLOCK
    w = ATTN_HEADS_PER_STEP * HEAD_DIM
    n_groups = d // w
    return pl.pallas_call(
        functools.partial(_attn_kernel, n_blocks=n_blocks),
        grid=(batch, n_groups),
        in_specs=[
            pl.BlockSpec((seq, w), lambda b, h: (b, n_groups + h)),
            pl.BlockSpec((seq, w), lambda b, h: (b, h)),
            pl.BlockSpec((w, seq), lambda b, h: (h, b)),
            pl.BlockSpec((n_blocks, w), lambda b, h: (b, h)),
        ],
        out_specs=pl.BlockSpec((seq, w), lambda b, h: (b, h)),
        out_shape=jax.ShapeDtypeStruct((batch * seq, d), BF16),
        scratch_shapes=[pltpu.VMEM((ATTN_LOOKAHEAD + 1, seq, MOBA_BLOCK), F32),
                        pltpu.VMEM((2, seq, MOBA_BLOCK), BF16)],
        compiler_params=_params("parallel", "arbitrary"),
        name="moba_attn",
    )(kq, kq, vt, km)


def kernel(x, a_norm, a_w_in, a_ln_g, a_ln_b, a_w_s, a_b_s, a_w_out, kv_norm, w_k, w_v, k_norm,
           b_norm, b_w_q, b_q_norm, b_w_o, ffn_norm, ffn_w_gate, ffn_w_up, ffn_w_down):
    batch, seq, d = x.shape
    assert seq % MOBA_BLOCK == 0 and seq // MOBA_BLOCK - 1 >= MOBA_TOPK
    n_a = a_norm.shape[0]
    depth = ffn_norm.shape[0]

    h = x.reshape(batch * seq, d)
    for layer in range(depth):
        if layer < n_a:
            i = layer
            ug = _sgu(h, a_norm[i], a_w_in, i, a_ln_g[i], a_ln_b[i], a_w_s[i], a_b_s[i])
            h = _mm_res(ug, a_w_out, i, h)
        else:
            i = layer - n_a
            assert depth - n_a == 1
            w_kvq = jnp.stack([w_k, w_v, b_w_q[i]]).astype(BF16)
            kq, vt, km = _kvq(h, kv_norm, b_norm[i], w_kvq, k_norm, b_q_norm[i])
            attn = _attention(kq, vt, km, batch, seq, d)
            h = _mm_res(attn, b_w_o, i, h)
        h = _ffn(h, ffn_norm[layer], ffn_w_gate, ffn_w_up, ffn_w_down, layer)
    return h.reshape(batch, seq, d)
```

```python
import functools

import jax
import jax.numpy as jnp
from jax import lax
from jax.experimental import pallas as pl
from jax.experimental.pallas import tpu as pltpu

F32 = jnp.float32
BF16 = jnp.bfloat16
EPS = 1e-6

SGU_CHUNK = 128
SGU_GROUPS = 16
HEAD_DIM = 128
MOBA_BLOCK = 256
MOBA_TOPK = 3

VMEM_LIMIT_BYTES = 56 * 1024 * 1024
FFN_VMEM_LIMIT_BYTES = 60 * 1024 * 1024
ROW_TILE = 1024
ROW_BLOCK = 256
NORM_ROWS = 128
FFN_ROW_TILE = 2048
FFN_ROW_BLOCK = 512
MM_RES_ROWS = 512
LOG2_E = 1.4426950408889634
ATTN_HEADS_PER_STEP = 4
ATTN_LOOKAHEAD = 3


def _params(*sem, vmem_limit_bytes=VMEM_LIMIT_BYTES):
    return pltpu.CompilerParams(dimension_semantics=sem, vmem_limit_bytes=vmem_limit_bytes)


def _rows(n_rows, block, body):
    def step(r, carry):
        body(pl.multiple_of(r * block, block))
        return carry
    lax.fori_loop(0, n_rows // block, step, 0)


def _rms_scale(x):
    return x * lax.rsqrt(jnp.mean(x * x, axis=-1, keepdims=True) + EPS)


def _sgu_kernel(x_ref, g_ref, w_ref, lng_ref, lnb_ref, ws_ref, bs_ref, o_ref,
                h_scr, u_scr, v_scr, wm_scr, *, tm, tn, n_u, n_v):
    j = pl.program_id(1)
    last = n_u + n_v - 1
    d = n_v * tn
    c = SGU_CHUNK

    def mask_weights():
        t_idx = lax.broadcasted_iota(jnp.int32, (c, c), 0)
        s_idx = lax.broadcasted_iota(jnp.int32, (c, c), 1)
        for g in range(SGU_GROUPS):
            wm_scr[g] = jnp.where(s_idx <= t_idx, ws_ref[g], 0.0).astype(BF16)

    def norm(r0):
        for r in range(r0, r0 + ROW_BLOCK, NORM_ROWS):
            x = x_ref[r:r + NORM_ROWS, :]
            h_scr[r:r + NORM_ROWS, :] = (_rms_scale(x) * g_ref[...]).astype(BF16)

    def gelu_mm(r0):
        z = jnp.dot(h_scr[r0:r0 + ROW_BLOCK, :], w_ref[...].astype(BF16),
                    preferred_element_type=F32)
        return jax.nn.gelu(z, approximate=True)

    def gate_rows(r0):
        for r in range(r0, r0 + ROW_BLOCK, c):
            parts = [v_scr[t, r:r + c, :] for t in range(n_v)]
            mu = sum(jnp.sum(p, axis=-1, keepdims=True) for p in parts) / d
            cen = [p - mu for p in parts]
            var = sum(jnp.sum(x * x, axis=-1, keepdims=True) for x in cen) / d
            inv = lax.rsqrt(var + EPS)
            for t in range(n_v):
                cols = slice(t * tn, (t + 1) * tn)
                v_ln = (cen[t] * inv * lng_ref[:, cols] + lnb_ref[:, cols]).astype(BF16)
                for g in range(t * tn // c, (t + 1) * tn // c):
                    off = g * c - t * tn
                    mixed = jnp.dot(wm_scr[g], v_ln[:, off:off + c], preferred_element_type=F32)
                    gate = mixed + bs_ref[g]
                    u = u_scr[t, r:r + c, off:off + c].astype(F32)
                    o_ref[r:r + c, g * c:(g + 1) * c] = (u * gate).astype(o_ref.dtype)

    def trips(first_step, u_step, last_step):
        for r0 in range(0, tm, ROW_BLOCK):
            if first_step:
                norm(r0)
            z = gelu_mm(r0)
            if u_step:
                u_scr[j, r0:r0 + ROW_BLOCK, :] = z.astype(BF16)
            elif last_step:
                v_scr[n_v - 1, r0:r0 + ROW_BLOCK, :] = z
                gate_rows(r0)
            else:
                v_scr[j - n_u, r0:r0 + ROW_BLOCK, :] = z

    @pl.when(j == 0)
    def _():
        mask_weights()
        trips(True, True, False)

    @pl.when(jnp.logical_and(j > 0, j < n_u))
    def _():
        trips(False, True, False)

    @pl.when(jnp.logical_and(j >= n_u, j < last))
    def _():
        trips(False, False, False)

    @pl.when(j == last)
    def _():
        trips(False, False, True)


def _sgu(x2d, norm_g, w_in, layer, ln_g, ln_b, w_s, b_s):
    t, d = x2d.shape
    d_sgu = w_in.shape[2] // 2
    tm, tn = ROW_TILE, 512
    n_u = n_v = d_sgu // tn
    assert d_sgu == SGU_GROUPS * SGU_CHUNK and tn % SGU_CHUNK == 0
    kern = functools.partial(_sgu_kernel, tm=tm, tn=tn, n_u=n_u, n_v=n_v)
    return pl.pallas_call(
        kern,
        grid=(t // tm, n_u + n_v),
        in_specs=[
            pl.BlockSpec((tm, d), lambda i, j: (i, 0)),
            pl.BlockSpec((1, d), lambda i, j: (0, 0)),
            pl.BlockSpec((None, d, tn), lambda i, j: (layer, 0, j)),
            pl.BlockSpec((1, d_sgu), lambda i, j: (0, 0)),
            pl.BlockSpec((1, d_sgu), lambda i, j: (0, 0)),
            pl.BlockSpec(w_s.shape, lambda i, j: (0, 0, 0)),
            pl.BlockSpec((SGU_GROUPS, SGU_CHUNK, 1), lambda i, j: (0, 0, 0)),
        ],
        out_specs=pl.BlockSpec((tm, d_sgu), lambda i, j: (i, 0)),
        out_shape=jax.ShapeDtypeStruct((t, d_sgu), BF16),
        scratch_shapes=[pltpu.VMEM((tm, d), BF16), pltpu.VMEM((n_u, tm, tn), BF16),
                        pltpu.VMEM((n_v, tm, tn), F32),
                        pltpu.VMEM((SGU_GROUPS, SGU_CHUNK, SGU_CHUNK), BF16)],
        compiler_params=_params("parallel", "arbitrary"),
        name="sgu",
    )(x2d, norm_g.reshape(1, d), w_in, ln_g.reshape(1, d_sgu), ln_b.reshape(1, d_sgu),
      w_s, b_s.reshape(SGU_GROUPS, SGU_CHUNK, 1))


def _mm_res_kernel(a_ref, w_ref, x_ref, o_ref, *, tn):
    for c0 in range(0, o_ref.shape[1], tn):
        cols = slice(c0, c0 + tn)
        acc = jnp.dot(a_ref[...], w_ref[:, cols].astype(BF16), preferred_element_type=F32)
        o_ref[:, cols] = x_ref[:, cols] + acc


def _mm_res(a, w_stack, layer, x2d):
    t, k = a.shape
    n = w_stack.shape[2]
    tm = MM_RES_ROWS
    return pl.pallas_call(
        functools.partial(_mm_res_kernel, tn=512),
        grid=(t // tm,),
        in_specs=[
            pl.BlockSpec((tm, k), lambda i: (i, 0)),
            pl.BlockSpec((None, k, n), lambda i: (layer, 0, 0), pipeline_mode=pl.Buffered(1)),
            pl.BlockSpec((tm, n), lambda i: (i, 0)),
        ],
        out_specs=pl.BlockSpec((tm, n), lambda i: (i, 0)),
        out_shape=jax.ShapeDtypeStruct((t, n), F32),
        compiler_params=_params("parallel"),
        name="mm_res",
    )(a, w_stack, x2d)


def _ffn_kernel(x_hbm, g_ref, wg_ref, wu_ref, wd_ref, o_hbm, acc, h_scr, wg_bf, wu_bf, wd_bf,
                in_sem, out_sem, *, tm, rb):
    i = pl.program_id(0)
    j = pl.program_id(1)
    n_i = pl.num_programs(0)
    n_j = pl.num_programs(1)
    n_blk = tm // rb

    def in_copy(tile, c):
        return pltpu.make_async_copy(x_hbm.at[pl.ds(tile * tm + c * rb, rb), :],
                                     acc.at[pl.ds(c * rb, rb), :], in_sem.at[c])

    def out_copy(tile, c):
        return pltpu.make_async_copy(acc.at[pl.ds(c * rb, rb), :],
                                     o_hbm.at[pl.ds(tile * tm + c * rb, rb), :], out_sem.at[c])

    @pl.when(j == 0)
    def _():
        @pl.when(i == 0)
        def _():
            for c in range(n_blk):
                in_copy(0, c).start()

        @pl.when(i > 0)
        def _():
            out_copy(i - 1, n_blk - 1).wait()
            in_copy(i, n_blk - 1).start()

        for c in range(n_blk):
            in_copy(i, c).wait()

            def norm(r0):
                rows = pl.ds(pl.multiple_of(c * rb + r0, NORM_ROWS), NORM_ROWS)
                h_scr[rows, :] = (_rms_scale(acc[rows, :]) * g_ref[...]).astype(BF16)
            _rows(rb, NORM_ROWS, norm)

    def trips(write_back):
        for r in range(n_blk):
            rows = pl.ds(r * rb, rb)
            hb = h_scr[rows, :]
            if r == 0:
                wg, wu, wd = (w[...].astype(BF16) for w in (wg_ref, wu_ref, wd_ref))
                wg_bf[...], wu_bf[...], wd_bf[...] = wg, wu, wd
            else:
                wg, wu, wd = wg_bf[...], wu_bf[...], wd_bf[...]
            gate = jnp.dot(hb, wg, preferred_element_type=F32)
            up = jnp.dot(hb, wu, preferred_element_type=F32)
            act = (jax.nn.silu(gate) * up).astype(BF16)
            acc[rows, :] += jnp.dot(act, wd, preferred_element_type=F32)
            if write_back:
                out_copy(i, r).start()
                if r > 0:
                    @pl.when(i < n_i - 1)
                    def _(r=r):
                        out_copy(i, r - 1).wait()
                        in_copy(i + 1, r - 1).start()

    @pl.when(j < n_j - 1)
    def _():
        trips(False)

    @pl.when(j == n_j - 1)
    def _():
        trips(True)

    @pl.when(jnp.logical_and(i == n_i - 1, j == n_j - 1))
    def _():
        for c in range(n_blk):
            out_copy(i, c).wait()


def _ffn(x2d, norm_g, w_gate, w_up, w_down, layer):
    t, d = x2d.shape
    f = w_gate.shape[2]
    tm, tf, rb = FFN_ROW_TILE, 512, FFN_ROW_BLOCK
    return pl.pallas_call(
        functools.partial(_ffn_kernel, tm=tm, rb=rb),
        grid=(t // tm, f // tf),
        in_specs=[
            pl.BlockSpec(memory_space=pl.ANY),
            pl.BlockSpec((1, d), lambda i, j: (0, 0)),
            pl.BlockSpec((None, d, tf), lambda i, j: (layer, 0, j)),
            pl.BlockSpec((None, d, tf), lambda i, j: (layer, 0, j)),
            pl.BlockSpec((None, tf, d), lambda i, j: (layer, j, 0)),
        ],
        out_specs=pl.BlockSpec(memory_space=pl.ANY),
        out_shape=jax.ShapeDtypeStruct((t, d), F32),
        scratch_shapes=[pltpu.VMEM((tm, d), F32), pltpu.VMEM((tm, d), BF16),
                        pltpu.VMEM((d, tf), BF16), pltpu.VMEM((d, tf), BF16),
                        pltpu.VMEM((tf, d), BF16),
                        pltpu.SemaphoreType.DMA((tm // rb,)), pltpu.SemaphoreType.DMA((tm // rb,))],
        compiler_params=_params("arbitrary", "arbitrary", vmem_limit_bytes=FFN_VMEM_LIMIT_BYTES),
        name="ffn",
    )(x2d, norm_g.reshape(1, d), w_gate, w_up, w_down)


def _kvq_kernel(x_ref, gkv_ref, gq_ref, w_ref, nk_ref, nq_ref, kq_ref, vt_ref, km_ref,
                hkv_scr, hq_scr, *, tm, tn, n_proj, q_scale):
    j = pl.program_id(1)
    heads = tn // HEAD_DIM
    blocks = tm // MOBA_BLOCK

    def norm(blk):
        for r in range(blk * MOBA_BLOCK, (blk + 1) * MOBA_BLOCK, NORM_ROWS):
            y = _rms_scale(x_ref[r:r + NORM_ROWS, :])
            hkv_scr[r:r + NORM_ROWS, :] = (y * gkv_ref[...]).astype(BF16)
            hq_scr[r:r + NORM_ROWS, :] = (y * gq_ref[...]).astype(BF16)

    def proj(h_scr, blk):
        rows = slice(blk * MOBA_BLOCK, (blk + 1) * MOBA_BLOCK)
        return rows, jnp.dot(h_scr[rows, :], w_ref[...], preferred_element_type=F32)

    def k_trips(first_step):
        for blk in range(blocks):
            if first_step:
                norm(blk)
            rows, y = proj(hkv_scr, blk)
            for hh in range(heads):
                cols = slice(hh * HEAD_DIM, (hh + 1) * HEAD_DIM)
                kh = _rms_scale(y[:, cols]) * nk_ref[...]
                kq_ref[rows, cols] = kh.astype(kq_ref.dtype)
                km_ref[0, blk:blk + 1, cols] = jnp.mean(kh, axis=0, keepdims=True)

    @pl.when(j == 0)
    def _():
        k_trips(True)

    @pl.when(jnp.logical_and(j > 0, j < n_proj))
    def _():
        k_trips(False)

    @pl.when(jnp.logical_and(j >= n_proj, j < 2 * n_proj))
    def _():
        for blk in range(blocks):
            rows, y = proj(hkv_scr, blk)
            vt_ref[:, rows] = y.T.astype(vt_ref.dtype)

    @pl.when(j >= 2 * n_proj)
    def _():
        for blk in range(blocks):
            rows, y = proj(hq_scr, blk)
            for hh in range(heads):
                cols = slice(hh * HEAD_DIM, (hh + 1) * HEAD_DIM)
                qh = _rms_scale(y[:, cols]) * nq_ref[...] * q_scale
                kq_ref[rows, cols] = qh.astype(kq_ref.dtype)


def _kvq(x2d, kv_norm, q_in_norm, w_kvq, k_norm, q_norm):
    t, d = x2d.shape
    tm, tn = ROW_TILE, 1024
    n_proj = d // tn
    blocks = tm // MOBA_BLOCK
    kern = functools.partial(_kvq_kernel, tm=tm, tn=tn, n_proj=n_proj, q_scale=HEAD_DIM ** -0.5 * LOG2_E)

    def kq_map(i, j):
        return i, jnp.where(j < n_proj, j, jnp.maximum(j - n_proj, n_proj - 1))

    kq, vt, km = pl.pallas_call(
        kern,
        grid=(t // tm, 3 * n_proj),
        in_specs=[
            pl.BlockSpec((tm, d), lambda i, j: (i, 0)),
            pl.BlockSpec((1, d), lambda i, j: (0, 0)),
            pl.BlockSpec((1, d), lambda i, j: (0, 0)),
            pl.BlockSpec((None, d, tn), lambda i, j: (j // n_proj, 0, j % n_proj)),
            pl.BlockSpec((1, HEAD_DIM), lambda i, j: (0, 0)),
            pl.BlockSpec((1, HEAD_DIM), lambda i, j: (0, 0)),
        ],
        out_specs=[
            pl.BlockSpec((tm, tn), kq_map),
            pl.BlockSpec((tn, tm), lambda i, j: (jnp.clip(j - n_proj, 0, n_proj - 1), i)),
            pl.BlockSpec((1, blocks, tn), lambda i, j: (i, 0, jnp.minimum(j, n_proj - 1))),
        ],
        out_shape=[jax.ShapeDtypeStruct((t, 2 * d), BF16),
                   jax.ShapeDtypeStruct((d, t), BF16),
                   jax.ShapeDtypeStruct((t // tm, blocks, d), F32)],
        scratch_shapes=[pltpu.VMEM((tm, d), BF16), pltpu.VMEM((tm, d), BF16)],
        compiler_params=_params("parallel", "arbitrary"),
        name="kvq",
    )(x2d, kv_norm.reshape(1, d), q_in_norm.reshape(1, d), w_kvq,
      k_norm.reshape(1, HEAD_DIM), q_norm.reshape(1, HEAD_DIM))
    return kq, vt, km.reshape(t // MOBA_BLOCK, d)


_NT = (((1,), (1,)), ((), ()))


def _attn_kernel(q_ref, k_ref, vt_ref, km_ref, o_ref, s_scr, p_scr, *, n_blocks):
    bs = MOBA_BLOCK
    sub = 8
    grp = bs // sub
    dh = HEAD_DIM
    n_heads = q_ref.shape[1] // dh
    blk_id = lax.broadcasted_iota(jnp.int32, (n_blocks, bs), 0)
    k_pos = lax.broadcasted_iota(jnp.int32, (bs, bs), 0)
    q_pos = lax.broadcasted_iota(jnp.int32, (bs, bs), 1)
    neg_inf = -jnp.inf
    causal = jnp.where(k_pos <= q_pos, 0.0, neg_inf).reshape(grp, sub, bs)

    n_slots = s_scr.shape[0]

    def scores(job):
        hd, i = jobs[job]
        cols = slice(hd * dh, (hd + 1) * dh)
        q = q_ref[i * bs:(i + 1) * bs, cols]
        nk = (i + 1) * bs
        s_scr[job % n_slots, 0:nk, :] = lax.dot_general(k_ref[0:nk, cols], q, _NT,
                                                        preferred_element_type=F32)
        if i == 0:
            return None
        km = km_ref[:, cols].astype(BF16)
        return lax.dot_general(km, q, _NT, preferred_element_type=F32)

    jobs = [(hd, i) for i in range(n_blocks - 1, -1, -1) for hd in range(n_heads)]
    gates = {job: scores(job) for job in range(min(ATTN_LOOKAHEAD, len(jobs)))}
    for job, (hd, i) in enumerate(jobs):
        cols = slice(hd * dh, (hd + 1) * dh)
        nk = (i + 1) * bs
        s_ref = s_scr.at[job % n_slots]
        p_ref = p_scr.at[job % 2]
        if job + ATTN_LOOKAHEAD < len(jobs):
            gates[job + ATTN_LOOKAHEAD] = scores(job + ATTN_LOOKAHEAD)
        gate = gates.pop(job)

        def piece(jb):
            return s_ref[jb * bs:(jb + 1) * bs, :].reshape(grp, sub, bs)

        biases = []
        if i > 0:
            rank = jnp.zeros((n_blocks, bs), F32)
            for m in range(i):
                gm = jnp.broadcast_to(gate[m:m + 1, :], (n_blocks, bs))
                beats = jnp.logical_or(gm > gate, jnp.logical_and(gm == gate, blk_id > m))
                rank = rank + jnp.where(beats, 1.0, 0.0)
            sel = jnp.logical_and(blk_id < i, rank < MOBA_TOPK)
            bias = jnp.where(sel, 0.0, neg_inf)
            biases = [jnp.broadcast_to(bias[jb:jb + 1, :], (sub, bs)) for jb in range(i)]
        m8 = jnp.max(piece(i) + causal, axis=0)
        for jb in range(i):
            m8 = jnp.maximum(m8, jnp.max(piece(jb), axis=0) + biases[jb])
        m_b = jnp.broadcast_to(jnp.max(m8, axis=0, keepdims=True), (sub, bs))
        l8 = jnp.zeros((sub, bs), F32)
        for jb in range(i + 1):
            if jb < i:
                p = jnp.exp2(piece(jb) - (m_b - biases[jb])[None])
            else:
                p = jnp.exp2(piece(jb) + causal - m_b[None])
            l8 = l8 + jnp.sum(p, axis=0)
            p_ref[jb * bs:(jb + 1) * bs, :] = p.reshape(bs, bs).astype(BF16)
        denom = jnp.sum(l8, axis=0, keepdims=True)
        o_t = jnp.dot(vt_ref[cols, 0:nk], p_ref[0:nk, :], preferred_element_type=F32)
        o_ref[i * bs:(i + 1) * bs, cols] = (o_t / denom).T.astype(o_ref.dtype)


def _attention(kq, vt, km, batch, seq, d):
    n_blocks = seq // MOBA_BLOCK
    w = ATTN_HEADS_PER_STEP * HEAD_DIM
    n_groups = d // w
    return pl.pallas_call(
        functools.partial(_attn_kernel, n_blocks=n_blocks),
        grid=(batch, n_groups),
        in_specs=[
            pl.BlockSpec((seq, w), lambda b, h: (b, n_groups + h)),
            pl.BlockSpec((seq, w), lambda b, h: (b, h)),
            pl.BlockSpec((w, seq), lambda b, h: (h, b)),
            pl.BlockSpec((n_blocks, w), lambda b, h: (b, h)),
        ],
        out_specs=pl.BlockSpec((seq, w), lambda b, h: (b, h)),
        out_shape=jax.ShapeDtypeStruct((batch * seq, d), BF16),
        scratch_shapes=[pltpu.VMEM((ATTN_LOOKAHEAD + 1, seq, MOBA_BLOCK), F32),
                        pltpu.VMEM((2, seq, MOBA_BLOCK), BF16)],
        compiler_params=_params("parallel", "arbitrary"),
        name="moba_attn",
    )(kq, kq, vt, km)


def kernel(x, a_norm, a_w_in, a_ln_g, a_ln_b, a_w_s, a_b_s, a_w_out, kv_norm, w_k, w_v, k_norm,
           b_norm, b_w_q, b_q_norm, b_w_o, ffn_norm, ffn_w_gate, ffn_w_up, ffn_w_down):
    batch, seq, d = x.shape
    assert seq % MOBA_BLOCK == 0 and seq // MOBA_BLOCK - 1 >= MOBA_TOPK
    n_a = a_norm.shape[0]
    depth = ffn_norm.shape[0]

    h = x.reshape(batch * seq, d)
    for layer in range(depth):
        if layer < n_a:
            i = layer
            ug = _sgu(h, a_norm[i], a_w_in, i, a_ln_g[i], a_ln_b[i], a_w_s[i], a_b_s[i])
            h = _mm_res(ug, a_w_out, i, h)
        else:
            i = layer - n_a
            assert depth - n_a == 1
            w_kvq = jnp.stack([w_k, w_v, b_w_q[i]]).astype(BF16)
            kq, vt, km = _kvq(h, kv_norm, b_norm[i], w_kvq, k_norm, b_q_norm[i])
            attn = _attention(kq, vt, km, batch, seq, d)
            h = _mm_res(attn, b_w_o, i, h)
        h = _ffn(h, ffn_norm[layer], ffn_w_gate, ffn_w_up, ffn_w_down, layer)
    return h.reshape(batch, seq, d)
```

```python
import functools

import jax
import jax.numpy as jnp
from jax import lax
from jax.experimental import pallas as pl
from jax.experimental.pallas import tpu as pltpu

F32 = jnp.float32
BF16 = jnp.bfloat16
EPS = 1e-6

SGU_CHUNK = 128
SGU_GROUPS = 16
HEAD_DIM = 128
MOBA_BLOCK = 256
MOBA_TOPK = 3

VMEM_LIMIT_BYTES = 56 * 1024 * 1024
LARGE_VMEM_LIMIT_BYTES = 60 * 1024 * 1024
ROW_TILE = 1024
ROW_BLOCK = 256
NORM_ROWS = 128
FFN_ROW_TILE = 2048
FFN_ROW_BLOCK = 512
MM_RES_ROWS = 512
LOG2_E = 1.4426950408889634
ATTN_HEADS_PER_STEP = 4
ATTN_LOOKAHEAD = 3


def _params(*sem, vmem_limit_bytes=VMEM_LIMIT_BYTES):
    return pltpu.CompilerParams(dimension_semantics=sem, vmem_limit_bytes=vmem_limit_bytes)


def _rows(n_rows, block, body):
    def step(r, carry):
        body(pl.multiple_of(r * block, block))
        return carry
    lax.fori_loop(0, n_rows // block, step, 0)


def _rms_scale(x):
    return x * lax.rsqrt(jnp.mean(x * x, axis=-1, keepdims=True) + EPS)


def _sgu_kernel(x_ref, g_ref, w_ref, lng_ref, lnb_ref, ws_ref, bs_ref, o_ref,
                h_scr, u_scr, v_scr, wm_scr, w_bf, *, tm, tn, n_u, n_v):
    j = pl.program_id(1)
    last = n_u + n_v - 1
    d = n_v * tn
    c = SGU_CHUNK

    def mask_weights():
        t_idx = lax.broadcasted_iota(jnp.int32, (c, c), 0)
        s_idx = lax.broadcasted_iota(jnp.int32, (c, c), 1)
        for g in range(SGU_GROUPS):
            wm_scr[g] = jnp.where(s_idx <= t_idx, ws_ref[g], 0.0).astype(BF16)

    def norm(r0):
        for r in range(r0, r0 + ROW_BLOCK, NORM_ROWS):
            x = x_ref[r:r + NORM_ROWS, :]
            h_scr[r:r + NORM_ROWS, :] = (_rms_scale(x) * g_ref[...]).astype(BF16)

    def gelu_mm(r0):
        if r0 == 0:
            w = w_ref[...].astype(BF16)
            w_bf[...] = w
        else:
            w = w_bf[...]
        z = jnp.dot(h_scr[r0:r0 + ROW_BLOCK, :], w, preferred_element_type=F32)
        return jax.nn.gelu(z, approximate=True)

    def gate_rows(r0):
        for r in range(r0, r0 + ROW_BLOCK, c):
            parts = [v_scr[t, r:r + c, :] for t in range(n_v)]
            mu = sum(jnp.sum(p, axis=-1, keepdims=True) for p in parts) / d
            cen = [p - mu for p in parts]
            var = sum(jnp.sum(x * x, axis=-1, keepdims=True) for x in cen) / d
            inv = lax.rsqrt(var + EPS)
            for t in range(n_v):
                cols = slice(t * tn, (t + 1) * tn)
                v_ln = (cen[t] * inv * lng_ref[:, cols] + lnb_ref[:, cols]).astype(BF16)
                for g in range(t * tn // c, (t + 1) * tn // c):
                    off = g * c - t * tn
                    mixed = jnp.dot(wm_scr[g], v_ln[:, off:off + c], preferred_element_type=F32)
                    gate = mixed + bs_ref[g]
                    u = u_scr[t, r:r + c, off:off + c].astype(F32)
                    o_ref[r:r + c, g * c:(g + 1) * c] = (u * gate).astype(o_ref.dtype)

    def trips(first_step, u_step, last_step):
        if last_step:
            z = gelu_mm(0)
            for r0 in range(0, tm, ROW_BLOCK):
                v_scr[n_v - 1, r0:r0 + ROW_BLOCK, :] = z
                if r0 + ROW_BLOCK < tm:
                    z = gelu_mm(r0 + ROW_BLOCK)
                gate_rows(r0)
            return
        for r0 in range(0, tm, ROW_BLOCK):
            if first_step:
                norm(r0)
            z = gelu_mm(r0)
            if u_step:
                u_scr[j, r0:r0 + ROW_BLOCK, :] = z.astype(BF16)
            else:
                v_scr[j - n_u, r0:r0 + ROW_BLOCK, :] = z

    @pl.when(j == 0)
    def _():
        mask_weights()
        trips(True, True, False)

    @pl.when(jnp.logical_and(j > 0, j < n_u))
    def _():
        trips(False, True, False)

    @pl.when(jnp.logical_and(j >= n_u, j < last))
    def _():
        trips(False, False, False)

    @pl.when(j == last)
    def _():
        trips(False, False, True)


def _sgu(x2d, norm_g, w_in, layer, ln_g, ln_b, w_s, b_s):
    t, d = x2d.shape
    d_sgu = w_in.shape[2] // 2
    tm, tn = ROW_TILE, 512
    n_u = n_v = d_sgu // tn
    assert d_sgu == SGU_GROUPS * SGU_CHUNK and tn % SGU_CHUNK == 0
    kern = functools.partial(_sgu_kernel, tm=tm, tn=tn, n_u=n_u, n_v=n_v)
    return pl.pallas_call(
        kern,
        grid=(t // tm, n_u + n_v),
        in_specs=[
            pl.BlockSpec((tm, d), lambda i, j: (i, 0)),
            pl.BlockSpec((1, d), lambda i, j: (0, 0)),
            pl.BlockSpec((None, d, tn), lambda i, j: (layer, 0, j)),
            pl.BlockSpec((1, d_sgu), lambda i, j: (0, 0)),
            pl.BlockSpec((1, d_sgu), lambda i, j: (0, 0)),
            pl.BlockSpec(w_s.shape, lambda i, j: (0, 0, 0)),
            pl.BlockSpec((SGU_GROUPS, SGU_CHUNK, 1), lambda i, j: (0, 0, 0)),
        ],
        out_specs=pl.BlockSpec((tm, d_sgu), lambda i, j: (i, 0)),
        out_shape=jax.ShapeDtypeStruct((t, d_sgu), BF16),
        scratch_shapes=[pltpu.VMEM((tm, d), BF16), pltpu.VMEM((n_u, tm, tn), BF16),
                        pltpu.VMEM((n_v, tm, tn), F32),
                        pltpu.VMEM((SGU_GROUPS, SGU_CHUNK, SGU_CHUNK), BF16),
                        pltpu.VMEM((d, tn), BF16)],
        compiler_params=_params("parallel", "arbitrary", vmem_limit_bytes=LARGE_VMEM_LIMIT_BYTES),
        name="sgu",
    )(x2d, norm_g.reshape(1, d), w_in, ln_g.reshape(1, d_sgu), ln_b.reshape(1, d_sgu),
      w_s, b_s.reshape(SGU_GROUPS, SGU_CHUNK, 1))


def _mm_res_kernel(a_ref, w_ref, x_ref, o_ref, w_bf, *, tn):
    def step(first):
        for c0 in range(0, o_ref.shape[1], tn):
            cols = slice(c0, c0 + tn)
            if first:
                w = w_ref[:, cols].astype(BF16)
                w_bf[:, cols] = w
            else:
                w = w_bf[:, cols]
            acc = jnp.dot(a_ref[...], w, preferred_element_type=F32)
            o_ref[:, cols] = x_ref[:, cols] + acc

    @pl.when(pl.program_id(0) == 0)
    def _():
        step(True)

    @pl.when(pl.program_id(0) > 0)
    def _():
        step(False)


def _mm_res(a, w_stack, layer, x2d):
    t, k = a.shape
    n = w_stack.shape[2]
    tm = MM_RES_ROWS
    return pl.pallas_call(
        functools.partial(_mm_res_kernel, tn=512),
        grid=(t // tm,),
        in_specs=[
            pl.BlockSpec((tm, k), lambda i: (i, 0)),
            pl.BlockSpec((None, k, n), lambda i: (layer, 0, 0), pipeline_mode=pl.Buffered(1)),
            pl.BlockSpec((tm, n), lambda i: (i, 0)),
        ],
        out_specs=pl.BlockSpec((tm, n), lambda i: (i, 0)),
        out_shape=jax.ShapeDtypeStruct((t, n), F32),
        scratch_shapes=[pltpu.VMEM((k, n), BF16)],
        compiler_params=_params("arbitrary"),
        name="mm_res",
    )(a, w_stack, x2d)


def _ffn_kernel(x_hbm, g_ref, wg_ref, wu_ref, wd_ref, o_hbm, acc, h_scr, wg_bf, wu_bf, wd_bf,
                in_sem, out_sem, *, tm, rb):
    i = pl.program_id(0)
    j = pl.program_id(1)
    n_i = pl.num_programs(0)
    n_j = pl.num_programs(1)
    n_blk = tm // rb

    def in_copy(tile, c):
        return pltpu.make_async_copy(x_hbm.at[pl.ds(tile * tm + c * rb, rb), :],
                                     acc.at[pl.ds(c * rb, rb), :], in_sem.at[c])

    def out_copy(tile, c):
        return pltpu.make_async_copy(acc.at[pl.ds(c * rb, rb), :],
                                     o_hbm.at[pl.ds(tile * tm + c * rb, rb), :], out_sem.at[c])

    @pl.when(j == 0)
    def _():
        @pl.when(i == 0)
        def _():
            for c in range(n_blk):
                in_copy(0, c).start()

        @pl.when(i > 0)
        def _():
            out_copy(i - 1, n_blk - 1).wait()
            in_copy(i, n_blk - 1).start()

        for c in range(n_blk):
            in_copy(i, c).wait()

            def norm(r0):
                rows = pl.ds(pl.multiple_of(c * rb + r0, NORM_ROWS), NORM_ROWS)
                h_scr[rows, :] = (_rms_scale(acc[rows, :]) * g_ref[...]).astype(BF16)
            _rows(rb, NORM_ROWS, norm)

    def trips(write_back):
        for r in range(n_blk):
            rows = pl.ds(r * rb, rb)
            hb = h_scr[rows, :]
            if r == 0:
                wg, wu, wd = (w[...].astype(BF16) for w in (wg_ref, wu_ref, wd_ref))
                wg_bf[...], wu_bf[...], wd_bf[...] = wg, wu, wd
            else:
                wg, wu, wd = wg_bf[...], wu_bf[...], wd_bf[...]
            gate = jnp.dot(hb, wg, preferred_element_type=F32)
            up = jnp.dot(hb, wu, preferred_element_type=F32)
            act = (jax.nn.silu(gate) * up).astype(BF16)
            acc[rows, :] += jnp.dot(act, wd, preferred_element_type=F32)
            if write_back:
                out_copy(i, r).start()
                if r > 0:
                    @pl.when(i < n_i - 1)
                    def _(r=r):
                        out_copy(i, r - 1).wait()
                        in_copy(i + 1, r - 1).start()

    @pl.when(j < n_j - 1)
    def _():
        trips(False)

    @pl.when(j == n_j - 1)
    def _():
        trips(True)

    @pl.when(jnp.logical_and(i == n_i - 1, j == n_j - 1))
    def _():
        for c in range(n_blk):
            out_copy(i, c).wait()


def _ffn(x2d, norm_g, w_gate, w_up, w_down, layer):
    t, d = x2d.shape
    f = w_gate.shape[2]
    tm, tf, rb = FFN_ROW_TILE, 512, FFN_ROW_BLOCK
    return pl.pallas_call(
        functools.partial(_ffn_kernel, tm=tm, rb=rb),
        grid=(t // tm, f // tf),
        in_specs=[
            pl.BlockSpec(memory_space=pl.ANY),
            pl.BlockSpec((1, d), lambda i, j: (0, 0)),
            pl.BlockSpec((None, d, tf), lambda i, j: (layer, 0, j)),
            pl.BlockSpec((None, d, tf), lambda i, j: (layer, 0, j)),
            pl.BlockSpec((None, tf, d), lambda i, j: (layer, j, 0)),
        ],
        out_specs=pl.BlockSpec(memory_space=pl.ANY),
        out_shape=jax.ShapeDtypeStruct((t, d), F32),
        scratch_shapes=[pltpu.VMEM((tm, d), F32), pltpu.VMEM((tm, d), BF16),
                        pltpu.VMEM((d, tf), BF16), pltpu.VMEM((d, tf), BF16),
                        pltpu.VMEM((tf, d), BF16),
                        pltpu.SemaphoreType.DMA((tm // rb,)), pltpu.SemaphoreType.DMA((tm // rb,))],
        compiler_params=_params("arbitrary", "arbitrary", vmem_limit_bytes=LARGE_VMEM_LIMIT_BYTES),
        name="ffn",
    )(x2d, norm_g.reshape(1, d), w_gate, w_up, w_down)


def _kvq_kernel(x_ref, gkv_ref, gq_ref, w_ref, nk_ref, nq_ref, kq_ref, vt_ref, km_ref,
                hkv_scr, hq_scr, *, tm, tn, n_proj, q_scale):
    j = pl.program_id(1)
    heads = tn // HEAD_DIM
    blocks = tm // MOBA_BLOCK

    def norm(blk):
        for r in range(blk * MOBA_BLOCK, (blk + 1) * MOBA_BLOCK, NORM_ROWS):
            y = _rms_scale(x_ref[r:r + NORM_ROWS, :])
            hkv_scr[r:r + NORM_ROWS, :] = (y * gkv_ref[...]).astype(BF16)
            hq_scr[r:r + NORM_ROWS, :] = (y * gq_ref[...]).astype(BF16)

    def proj(h_scr, blk):
        rows = slice(blk * MOBA_BLOCK, (blk + 1) * MOBA_BLOCK)
        return rows, jnp.dot(h_scr[rows, :], w_ref[...], preferred_element_type=F32)

    def k_trips(first_step):
        for blk in range(blocks):
            if first_step:
                norm(blk)
            rows, y = proj(hkv_scr, blk)
            for hh in range(heads):
                cols = slice(hh * HEAD_DIM, (hh + 1) * HEAD_DIM)
                kh = _rms_scale(y[:, cols]) * nk_ref[...]
                kq_ref[rows, cols] = kh.astype(kq_ref.dtype)
                km_ref[0, blk:blk + 1, cols] = jnp.mean(kh, axis=0, keepdims=True)

    @pl.when(j == 0)
    def _():
        k_trips(True)

    @pl.when(jnp.logical_and(j > 0, j < n_proj))
    def _():
        k_trips(False)

    @pl.when(jnp.logical_and(j >= n_proj, j < 2 * n_proj))
    def _():
        for blk in range(blocks):
            rows, y = proj(hkv_scr, blk)
            vt_ref[:, rows] = y.T.astype(vt_ref.dtype)

    @pl.when(j >= 2 * n_proj)
    def _():
        for blk in range(blocks):
            rows, y = proj(hq_scr, blk)
            for hh in range(heads):
                cols = slice(hh * HEAD_DIM, (hh + 1) * HEAD_DIM)
                qh = _rms_scale(y[:, cols]) * nq_ref[...] * q_scale
                kq_ref[rows, cols] = qh.astype(kq_ref.dtype)


def _kvq(x2d, kv_norm, q_in_norm, w_kvq, k_norm, q_norm):
    t, d = x2d.shape
    tm, tn = ROW_TILE, 1024
    n_proj = d // tn
    blocks = tm // MOBA_BLOCK
    kern = functools.partial(_kvq_kernel, tm=tm, tn=tn, n_proj=n_proj, q_scale=HEAD_DIM ** -0.5 * LOG2_E)

    def kq_map(i, j):
        return i, jnp.where(j < n_proj, j, jnp.maximum(j - n_proj, n_proj - 1))

    kq, vt, km = pl.pallas_call(
        kern,
        grid=(t // tm, 3 * n_proj),
        in_specs=[
            pl.BlockSpec((tm, d), lambda i, j: (i, 0)),
            pl.BlockSpec((1, d), lambda i, j: (0, 0)),
            pl.BlockSpec((1, d), lambda i, j: (0, 0)),
            pl.BlockSpec((None, d, tn), lambda i, j: (j // n_proj, 0, j % n_proj)),
            pl.BlockSpec((1, HEAD_DIM), lambda i, j: (0, 0)),
            pl.BlockSpec((1, HEAD_DIM), lambda i, j: (0, 0)),
        ],
        out_specs=[
            pl.BlockSpec((tm, tn), kq_map),
            pl.BlockSpec((tn, tm), lambda i, j: (jnp.clip(j - n_proj, 0, n_proj - 1), i)),
            pl.BlockSpec((1, blocks, tn), lambda i, j: (i, 0, jnp.minimum(j, n_proj - 1))),
        ],
        out_shape=[jax.ShapeDtypeStruct((t, 2 * d), BF16),
                   jax.ShapeDtypeStruct((d, t), BF16),
                   jax.ShapeDtypeStruct((t // tm, blocks, d), F32)],
        scratch_shapes=[pltpu.VMEM((tm, d), BF16), pltpu.VMEM((tm, d), BF16)],
        compiler_params=_params("parallel", "arbitrary"),
        name="kvq",
    )(x2d, kv_norm.reshape(1, d), q_in_norm.reshape(1, d), w_kvq,
      k_norm.reshape(1, HEAD_DIM), q_norm.reshape(1, HEAD_DIM))
    return kq, vt, km.reshape(t // MOBA_BLOCK, d)


_NT = (((1,), (1,)), ((), ()))


def _attn_kernel(q_ref, k_ref, vt_ref, km_ref, o_ref, s_scr, p_scr, *, n_blocks):
    bs = MOBA_BLOCK
    sub = 8
    grp = bs // sub
    dh = HEAD_DIM
    n_heads = q_ref.shape[1] // dh
    blk_id = lax.broadcasted_iota(jnp.int32, (n_blocks, bs), 0)
    k_pos = lax.broadcasted_iota(jnp.int32, (bs, bs), 0)
    q_pos = lax.broadcasted_iota(jnp.int32, (bs, bs), 1)
    neg_inf = -jnp.inf
    causal = jnp.where(k_pos <= q_pos, 0.0, neg_inf).reshape(grp, sub, bs)

    n_slots = s_scr.shape[0]

    def scores(job):
        hd, i = jobs[job]
        cols = slice(hd * dh, (hd + 1) * dh)
        q = q_ref[i * bs:(i + 1) * bs, cols]
        nk = (i + 1) * bs
        s_scr[job % n_slots, 0:nk, :] = lax.dot_general(k_ref[0:nk, cols], q, _NT,
                                                        preferred_element_type=F32)
        if i == 0:
            return None
        km = km_ref[:, cols].astype(BF16)
        return lax.dot_general(km, q, _NT, preferred_element_type=F32)

    jobs = [(hd, i) for i in range(n_blocks - 1, -1, -1) for hd in range(n_heads)]
    gates = {job: scores(job) for job in range(min(ATTN_LOOKAHEAD, len(jobs)))}
    for job, (hd, i) in enumerate(jobs):
        cols = slice(hd * dh, (hd + 1) * dh)
        nk = (i + 1) * bs
        s_ref = s_scr.at[job % n_slots]
        p_ref = p_scr.at[job % 2]
        if job + ATTN_LOOKAHEAD < len(jobs):
            gates[job + ATTN_LOOKAHEAD] = scores(job + ATTN_LOOKAHEAD)
        gate = gates.pop(job)

        def piece(jb):
            return s_ref[jb * bs:(jb + 1) * bs, :].reshape(grp, sub, bs)

        biases = []
        if i > 0:
            rank = jnp.zeros((n_blocks, bs), F32)
            for m in range(i):
                gm = jnp.broadcast_to(gate[m:m + 1, :], (n_blocks, bs))
                beats = jnp.logical_or(gm > gate, jnp.logical_and(gm == gate, blk_id > m))
                rank = rank + jnp.where(beats, 1.0, 0.0)
            sel = jnp.logical_and(blk_id < i, rank < MOBA_TOPK)
            bias = jnp.where(sel, 0.0, neg_inf)
            biases = [jnp.broadcast_to(bias[jb:jb + 1, :], (sub, bs)) for jb in range(i)]
        m8 = jnp.max(piece(i) + causal, axis=0)
        for jb in range(i):
            m8 = jnp.maximum(m8, jnp.max(piece(jb), axis=0) + biases[jb])
        m_b = jnp.broadcast_to(jnp.max(m8, axis=0, keepdims=True), (sub, bs))
        l8 = jnp.zeros((sub, bs), F32)
        for jb in range(i + 1):
            if jb < i:
                p = jnp.exp2(piece(jb) - (m_b - biases[jb])[None])
            else:
                p = jnp.exp2(piece(jb) + causal - m_b[None])
            l8 = l8 + jnp.sum(p, axis=0)
            p_ref[jb * bs:(jb + 1) * bs, :] = p.reshape(bs, bs).astype(BF16)
        denom = jnp.sum(l8, axis=0, keepdims=True)
        o_t = jnp.dot(vt_ref[cols, 0:nk], p_ref[0:nk, :], preferred_element_type=F32)
        o_ref[i * bs:(i + 1) * bs, cols] = (o_t / denom).T.astype(o_ref.dtype)


def _attention(kq, vt, km, batch, seq, d):
    n_blocks = seq // MOBA_BLOCK
    w = ATTN_HEADS_PER_STEP * HEAD_DIM
    n_groups = d // w
    return pl.pallas_call(
        functools.partial(_attn_kernel, n_blocks=n_blocks),
        grid=(batch, n_groups),
        in_specs=[
            pl.BlockSpec((seq, w), lambda b, h: (b, n_groups + h)),
            pl.BlockSpec((seq, w), lambda b, h: (b, h)),
            pl.BlockSpec((w, seq), lambda b, h: (h, b)),
            pl.BlockSpec((n_blocks, w), lambda b, h: (b, h)),
        ],
        out_specs=pl.BlockSpec((seq, w), lambda b, h: (b, h)),
        out_shape=jax.ShapeDtypeStruct((batch * seq, d), BF16),
        scratch_shapes=[pltpu.VMEM((ATTN_LOOKAHEAD + 1, seq, MOBA_BLOCK), F32),
                        pltpu.VMEM((2, seq, MOBA_BLOCK), BF16)],
        compiler_params=_params("parallel", "arbitrary"),
        name="moba_attn",
    )(kq, kq, vt, km)


def kernel(x, a_norm, a_w_in, a_ln_g, a_ln_b, a_w_s, a_b_s, a_w_out, kv_norm, w_k, w_v, k_norm,
           b_norm, b_w_q, b_q_norm, b_w_o, ffn_norm, ffn_w_gate, ffn_w_up, ffn_w_down):
    batch, seq, d = x.shape
    assert seq % MOBA_BLOCK == 0 and seq // MOBA_BLOCK - 1 >= MOBA_TOPK
    n_a = a_norm.shape[0]
    depth = ffn_norm.shape[0]

    h = x.reshape(batch * seq, d)
    for layer in range(depth):
        if layer < n_a:
            i = layer
            ug = _sgu(h, a_norm[i], a_w_in, i, a_ln_g[i], a_ln_b[i], a_w_s[i], a_b_s[i])
            h = _mm_res(ug, a_w_out, i, h)
        else:
            i = layer - n_a
            assert depth - n_a == 1
            w_kvq = jnp.stack([w_k, w_v, b_w_q[i]]).astype(BF16)
            kq, vt, km = _kvq(h, kv_norm, b_norm[i], w_kvq, k_norm, b_q_norm[i])
            attn = _attention(kq, vt, km, batch, seq, d)
            h = _mm_res(attn, b_w_o, i, h)
        h = _ffn(h, ffn_norm[layer], ffn_w_gate, ffn_w_up, ffn_w_down, layer)
    return h.reshape(batch, seq, d)
```

```python
import functools

import jax
import jax.numpy as jnp
from jax import lax
from jax.experimental import pallas as pl
from jax.experimental.pallas import tpu as pltpu

F32 = jnp.float32
BF16 = jnp.bfloat16
EPS = 1e-6

SGU_CHUNK = 128
SGU_GROUPS = 16
HEAD_DIM = 128
MOBA_BLOCK = 256
MOBA_TOPK = 3

VMEM_LIMIT_BYTES = 56 * 1024 * 1024
LARGE_VMEM_LIMIT_BYTES = 60 * 1024 * 1024
ROW_TILE = 1024
ROW_BLOCK = 256
NORM_ROWS = 128
FFN_ROW_TILE = 2048
FFN_ROW_BLOCK = 512
MM_RES_ROWS = 512
LOG2_E = 1.4426950408889634
ATTN_HEADS_PER_STEP = 4
ATTN_LOOKAHEAD = 3


def _params(*sem, vmem_limit_bytes=VMEM_LIMIT_BYTES):
    return pltpu.CompilerParams(dimension_semantics=sem, vmem_limit_bytes=vmem_limit_bytes)


def _rows(n_rows, block, body):
    def step(r, carry):
        body(pl.multiple_of(r * block, block))
        return carry
    lax.fori_loop(0, n_rows // block, step, 0)


def _rms_scale(x):
    return x * lax.rsqrt(jnp.mean(x * x, axis=-1, keepdims=True) + EPS)


def _sgu_kernel(x_ref, g_ref, w_ref, lng_ref, lnb_ref, ws_ref, bs_ref, o_ref,
                h_scr, u_scr, v_scr, wm_scr, w_bf, *, tm, tn, n_u, n_v):
    j = pl.program_id(1)
    last = n_u + n_v - 1
    d = n_v * tn
    c = SGU_CHUNK

    def mask_weights():
        t_idx = lax.broadcasted_iota(jnp.int32, (c, c), 0)
        s_idx = lax.broadcasted_iota(jnp.int32, (c, c), 1)
        for g in range(SGU_GROUPS):
            wm_scr[g] = jnp.where(s_idx <= t_idx, ws_ref[g], 0.0).astype(BF16)

    def norm(r0):
        for r in range(r0, r0 + ROW_BLOCK, NORM_ROWS):
            x = x_ref[r:r + NORM_ROWS, :]
            h_scr[r:r + NORM_ROWS, :] = (_rms_scale(x) * g_ref[...]).astype(BF16)

    def gelu_mm(r0):
        if r0 == 0:
            w = w_ref[...].astype(BF16)
            w_bf[...] = w
        else:
            w = w_bf[...]
        z = jnp.dot(h_scr[r0:r0 + ROW_BLOCK, :], w, preferred_element_type=F32)
        return jax.nn.gelu(z, approximate=True)

    def gate_rows(r0):
        for r in range(r0, r0 + ROW_BLOCK, c):
            parts = [v_scr[t, r:r + c, :] for t in range(n_v)]
            mu = sum(jnp.sum(p, axis=-1, keepdims=True) for p in parts) / d
            cen = [p - mu for p in parts]
            var = sum(jnp.sum(x * x, axis=-1, keepdims=True) for x in cen) / d
            inv = lax.rsqrt(var + EPS)
            for t in range(n_v):
                cols = slice(t * tn, (t + 1) * tn)
                v_ln = (cen[t] * inv * lng_ref[:, cols] + lnb_ref[:, cols]).astype(BF16)
                for g in range(t * tn // c, (t + 1) * tn // c):
                    off = g * c - t * tn
                    mixed = jnp.dot(wm_scr[g], v_ln[:, off:off + c], preferred_element_type=F32)
                    gate = mixed + bs_ref[g]
                    u = u_scr[t, r:r + c, off:off + c].astype(F32)
                    o_ref[r:r + c, g * c:(g + 1) * c] = (u * gate).astype(o_ref.dtype)

    def trips(first_step, u_step, last_step):
        if last_step:
            z = gelu_mm(0)
            for r0 in range(0, tm, ROW_BLOCK):
                v_scr[n_v - 1, r0:r0 + ROW_BLOCK, :] = z
                if r0 + ROW_BLOCK < tm:
                    z = gelu_mm(r0 + ROW_BLOCK)
                gate_rows(r0)
            return
        for r0 in range(0, tm, ROW_BLOCK):
            if first_step:
                norm(r0)
            z = gelu_mm(r0)
            if u_step:
                u_scr[j, r0:r0 + ROW_BLOCK, :] = z.astype(BF16)
            else:
                v_scr[j - n_u, r0:r0 + ROW_BLOCK, :] = z

    @pl.when(j == 0)
    def _():
        mask_weights()
        trips(True, True, False)

    @pl.when(jnp.logical_and(j > 0, j < n_u))
    def _():
        trips(False, True, False)

    @pl.when(jnp.logical_and(j >= n_u, j < last))
    def _():
        trips(False, False, False)

    @pl.when(j == last)
    def _():
        trips(False, False, True)


def _sgu(x2d, norm_g, w_in, layer, ln_g, ln_b, w_s, b_s):
    t, d = x2d.shape
    d_sgu = w_in.shape[2] // 2
    tm, tn = ROW_TILE, 512
    n_u = n_v = d_sgu // tn
    assert d_sgu == SGU_GROUPS * SGU_CHUNK and tn % SGU_CHUNK == 0
    kern = functools.partial(_sgu_kernel, tm=tm, tn=tn, n_u=n_u, n_v=n_v)
    return pl.pallas_call(
        kern,
        grid=(t // tm, n_u + n_v),
        in_specs=[
            pl.BlockSpec((tm, d), lambda i, j: (i, 0)),
            pl.BlockSpec((1, d), lambda i, j: (0, 0)),
            pl.BlockSpec((None, d, tn), lambda i, j: (layer, 0, j)),
            pl.BlockSpec((1, d_sgu), lambda i, j: (0, 0)),
            pl.BlockSpec((1, d_sgu), lambda i, j: (0, 0)),
            pl.BlockSpec(w_s.shape, lambda i, j: (0, 0, 0)),
            pl.BlockSpec((SGU_GROUPS, SGU_CHUNK, 1), lambda i, j: (0, 0, 0)),
        ],
        out_specs=pl.BlockSpec((tm, d_sgu), lambda i, j: (i, 0)),
        out_shape=jax.ShapeDtypeStruct((t, d_sgu), BF16),
        scratch_shapes=[pltpu.VMEM((tm, d), BF16), pltpu.VMEM((n_u, tm, tn), BF16),
                        pltpu.VMEM((n_v, tm, tn), F32),
                        pltpu.VMEM((SGU_GROUPS, SGU_CHUNK, SGU_CHUNK), BF16),
                        pltpu.VMEM((d, tn), BF16)],
        compiler_params=_params("parallel", "arbitrary", vmem_limit_bytes=LARGE_VMEM_LIMIT_BYTES),
        name="sgu",
    )(x2d, norm_g.reshape(1, d), w_in, ln_g.reshape(1, d_sgu), ln_b.reshape(1, d_sgu),
      w_s, b_s.reshape(SGU_GROUPS, SGU_CHUNK, 1))


def _mm_res_kernel(a_ref, w_ref, x_ref, o_ref, w_bf, *, tn):
    def step(first):
        for c0 in range(0, o_ref.shape[1], tn):
            cols = slice(c0, c0 + tn)
            if first:
                w = w_ref[:, cols].astype(BF16)
                w_bf[:, cols] = w
            else:
                w = w_bf[:, cols]
            acc = jnp.dot(a_ref[...], w, preferred_element_type=F32)
            o_ref[:, cols] = x_ref[:, cols] + acc

    @pl.when(pl.program_id(0) == 0)
    def _():
        step(True)

    @pl.when(pl.program_id(0) > 0)
    def _():
        step(False)


def _mm_res(a, w_stack, layer, x2d):
    t, k = a.shape
    n = w_stack.shape[2]
    tm = MM_RES_ROWS
    return pl.pallas_call(
        functools.partial(_mm_res_kernel, tn=512),
        grid=(t // tm,),
        in_specs=[
            pl.BlockSpec((tm, k), lambda i: (i, 0)),
            pl.BlockSpec((None, k, n), lambda i: (layer, 0, 0), pipeline_mode=pl.Buffered(1)),
            pl.BlockSpec((tm, n), lambda i: (i, 0)),
        ],
        out_specs=pl.BlockSpec((tm, n), lambda i: (i, 0)),
        out_shape=jax.ShapeDtypeStruct((t, n), F32),
        scratch_shapes=[pltpu.VMEM((k, n), BF16)],
        compiler_params=_params("arbitrary"),
        name="mm_res",
    )(a, w_stack, x2d)


def _ffn_kernel(x_hbm, g_ref, wg_ref, wu_ref, wd_ref, o_hbm, acc, h_scr, wg_bf, wu_bf, wd_bf,
                in_sem, out_sem, *, tm, rb):
    i = pl.program_id(0)
    j = pl.program_id(1)
    n_i = pl.num_programs(0)
    n_j = pl.num_programs(1)
    n_blk = tm // rb

    def in_copy(tile, c):
        return pltpu.make_async_copy(x_hbm.at[pl.ds(tile * tm + c * rb, rb), :],
                                     acc.at[pl.ds(c * rb, rb), :], in_sem.at[c])

    def out_copy(tile, c):
        return pltpu.make_async_copy(acc.at[pl.ds(c * rb, rb), :],
                                     o_hbm.at[pl.ds(tile * tm + c * rb, rb), :], out_sem.at[c])

    @pl.when(j == 0)
    def _():
        @pl.when(i == 0)
        def _():
            for c in range(n_blk):
                in_copy(0, c).start()

        @pl.when(i > 0)
        def _():
            out_copy(i - 1, n_blk - 1).wait()
            in_copy(i, n_blk - 1).start()

        for c in range(n_blk):
            in_copy(i, c).wait()

            def norm(r0):
                rows = pl.ds(pl.multiple_of(c * rb + r0, NORM_ROWS), NORM_ROWS)
                h_scr[rows, :] = (_rms_scale(acc[rows, :]) * g_ref[...]).astype(BF16)
            _rows(rb, NORM_ROWS, norm)

    def trips(write_back):
        for r in range(n_blk):
            rows = pl.ds(r * rb, rb)
            hb = h_scr[rows, :]
            if r == 0:
                wg, wu, wd = (w[...].astype(BF16) for w in (wg_ref, wu_ref, wd_ref))
                wg_bf[...], wu_bf[...], wd_bf[...] = wg, wu, wd
            else:
                wg, wu, wd = wg_bf[...], wu_bf[...], wd_bf[...]
            gate = jnp.dot(hb, wg, preferred_element_type=F32)
            up = jnp.dot(hb, wu, preferred_element_type=F32)
            act = (jax.nn.silu(gate) * up).astype(BF16)
            acc[rows, :] += jnp.dot(act, wd, preferred_element_type=F32)
            if write_back:
                out_copy(i, r).start()
                if r > 0:
                    @pl.when(i < n_i - 1)
                    def _(r=r):
                        out_copy(i, r - 1).wait()
                        in_copy(i + 1, r - 1).start()

    @pl.when(j < n_j - 1)
    def _():
        trips(False)

    @pl.when(j == n_j - 1)
    def _():
        trips(True)

    @pl.when(jnp.logical_and(i == n_i - 1, j == n_j - 1))
    def _():
        for c in range(n_blk):
            out_copy(i, c).wait()


def _ffn(x2d, norm_g, w_gate, w_up, w_down, layer):
    t, d = x2d.shape
    f = w_gate.shape[2]
    tm, tf, rb = FFN_ROW_TILE, 512, FFN_ROW_BLOCK
    return pl.pallas_call(
        functools.partial(_ffn_kernel, tm=tm, rb=rb),
        grid=(t // tm, f // tf),
        in_specs=[
            pl.BlockSpec(memory_space=pl.ANY),
            pl.BlockSpec((1, d), lambda i, j: (0, 0)),
            pl.BlockSpec((None, d, tf), lambda i, j: (layer, 0, j)),
            pl.BlockSpec((None, d, tf), lambda i, j: (layer, 0, j)),
            pl.BlockSpec((None, tf, d), lambda i, j: (layer, j, 0)),
        ],
        out_specs=pl.BlockSpec(memory_space=pl.ANY),
        out_shape=jax.ShapeDtypeStruct((t, d), F32),
        scratch_shapes=[pltpu.VMEM((tm, d), F32), pltpu.VMEM((tm, d), BF16),
                        pltpu.VMEM((d, tf), BF16), pltpu.VMEM((d, tf), BF16),
                        pltpu.VMEM((tf, d), BF16),
                        pltpu.SemaphoreType.DMA((tm // rb,)), pltpu.SemaphoreType.DMA((tm // rb,))],
        compiler_params=_params("arbitrary", "arbitrary", vmem_limit_bytes=LARGE_VMEM_LIMIT_BYTES),
        name="ffn",
    )(x2d, norm_g.reshape(1, d), w_gate, w_up, w_down)


def _kvq_kernel(x_ref, gkv_ref, gq_ref, w_ref, nk_ref, nq_ref, kq_ref, vt_ref, km_ref,
                hkv_scr, hq_scr, *, tm, tn, n_proj, q_scale):
    j = pl.program_id(1)
    heads = tn // HEAD_DIM
    blocks = tm // MOBA_BLOCK

    def norm(blk):
        for r in range(blk * MOBA_BLOCK, (blk + 1) * MOBA_BLOCK, NORM_ROWS):
            y = _rms_scale(x_ref[r:r + NORM_ROWS, :])
            hkv_scr[r:r + NORM_ROWS, :] = (y * gkv_ref[...]).astype(BF16)
            hq_scr[r:r + NORM_ROWS, :] = (y * gq_ref[...]).astype(BF16)

    def proj(h_scr, blk):
        rows = slice(blk * MOBA_BLOCK, (blk + 1) * MOBA_BLOCK)
        return rows, jnp.dot(h_scr[rows, :], w_ref[...], preferred_element_type=F32)

    def k_trips(first_step):
        for blk in range(blocks):
            if first_step:
                norm(blk)
            rows, y = proj(hkv_scr, blk)
            for hh in range(heads):
                cols = slice(hh * HEAD_DIM, (hh + 1) * HEAD_DIM)
                kh = _rms_scale(y[:, cols]) * nk_ref[...]
                kq_ref[rows, cols] = kh.astype(kq_ref.dtype)
                km_ref[0, blk:blk + 1, cols] = jnp.mean(kh, axis=0, keepdims=True)

    @pl.when(j == 0)
    def _():
        k_trips(True)

    @pl.when(jnp.logical_and(j > 0, j < n_proj))
    def _():
        k_trips(False)

    @pl.when(jnp.logical_and(j >= n_proj, j < 2 * n_proj))
    def _():
        for blk in range(blocks):
            rows, y = proj(hkv_scr, blk)
            vt_ref[:, rows] = y.T.astype(vt_ref.dtype)

    @pl.when(j >= 2 * n_proj)
    def _():
        for blk in range(blocks):
            rows, y = proj(hq_scr, blk)
            for hh in range(heads):
                cols = slice(hh * HEAD_DIM, (hh + 1) * HEAD_DIM)
                qh = _rms_scale(y[:, cols]) * nq_ref[...] * q_scale
                kq_ref[rows, cols] = qh.astype(kq_ref.dtype)


def _kvq(x2d, kv_norm, q_in_norm, w_kvq, k_norm, q_norm):
    t, d = x2d.shape
    tm, tn = ROW_TILE, 1024
    n_proj = d // tn
    blocks = tm // MOBA_BLOCK
    kern = functools.partial(_kvq_kernel, tm=tm, tn=tn, n_proj=n_proj, q_scale=HEAD_DIM ** -0.5 * LOG2_E)

    def kq_map(i, j):
        return i, jnp.where(j < n_proj, j, jnp.maximum(j - n_proj, n_proj - 1))

    kq, vt, km = pl.pallas_call(
        kern,
        grid=(t // tm, 3 * n_proj),
        in_specs=[
            pl.BlockSpec((tm, d), lambda i, j: (i, 0)),
            pl.BlockSpec((1, d), lambda i, j: (0, 0)),
            pl.BlockSpec((1, d), lambda i, j: (0, 0)),
            pl.BlockSpec((None, d, tn), lambda i, j: (j // n_proj, 0, j % n_proj)),
            pl.BlockSpec((1, HEAD_DIM), lambda i, j: (0, 0)),
            pl.BlockSpec((1, HEAD_DIM), lambda i, j: (0, 0)),
        ],
        out_specs=[
            pl.BlockSpec((tm, tn), kq_map),
            pl.BlockSpec((tn, tm), lambda i, j: (jnp.clip(j - n_proj, 0, n_proj - 1), i)),
            pl.BlockSpec((1, blocks, tn), lambda i, j: (i, 0, jnp.minimum(j, n_proj - 1))),
        ],
        out_shape=[jax.ShapeDtypeStruct((t, 2 * d), BF16),
                   jax.ShapeDtypeStruct((d, t), BF16),
                   jax.ShapeDtypeStruct((t // tm, blocks, d), F32)],
        scratch_shapes=[pltpu.VMEM((tm, d), BF16), pltpu.VMEM((tm, d), BF16)],
        compiler_params=_params("parallel", "arbitrary"),
        name="kvq",
    )(x2d, kv_norm.reshape(1, d), q_in_norm.reshape(1, d), w_kvq,
      k_norm.reshape(1, HEAD_DIM), q_norm.reshape(1, HEAD_DIM))
    return kq, vt, km.reshape(t // MOBA_BLOCK, d)


_NT = (((1,), (1,)), ((), ()))


def _attn_kernel(q_ref, k_ref, vt_ref, km_ref, o_ref, s_scr, p_scr, *, n_blocks):
    bs = MOBA_BLOCK
    sub = 8
    grp = bs // sub
    dh = HEAD_DIM
    n_heads = q_ref.shape[1] // dh
    blk_id = lax.broadcasted_iota(jnp.int32, (n_blocks, bs), 0)
    k_pos = lax.broadcasted_iota(jnp.int32, (bs, bs), 0)
    q_pos = lax.broadcasted_iota(jnp.int32, (bs, bs), 1)
    neg_inf = -jnp.inf
    causal = jnp.where(k_pos <= q_pos, 0.0, neg_inf).reshape(grp, sub, bs)

    n_slots = s_scr.shape[0]

    def scores(job):
        hd, i = jobs[job]
        cols = slice(hd * dh, (hd + 1) * dh)
        q = q_ref[i * bs:(i + 1) * bs, cols]
        nk = (i + 1) * bs
        s_scr[job % n_slots, 0:nk, :] = lax.dot_general(k_ref[0:nk, cols], q, _NT,
                                                        preferred_element_type=F32)
        if i <= MOBA_TOPK:
            return None
        km = km_ref[:, cols].astype(BF16)
        return lax.dot_general(km, q, _NT, preferred_element_type=F32)

    jobs = [(hd, i) for i in range(n_blocks - 1, -1, -1) for hd in range(n_heads)]
    gates = {job: scores(job) for job in range(min(ATTN_LOOKAHEAD, len(jobs)))}
    for job, (hd, i) in enumerate(jobs):
        cols = slice(hd * dh, (hd + 1) * dh)
        nk = (i + 1) * bs
        s_ref = s_scr.at[job % n_slots]
        p_ref = p_scr.at[job % 2]
        if job + ATTN_LOOKAHEAD < len(jobs):
            gates[job + ATTN_LOOKAHEAD] = scores(job + ATTN_LOOKAHEAD)
        gate = gates.pop(job)

        def piece(jb):
            return s_ref[jb * bs:(jb + 1) * bs, :].reshape(grp, sub, bs)

        ranked = i > MOBA_TOPK
        biases = []
        if ranked:
            rank = jnp.zeros((n_blocks, bs), F32)
            for m in range(i):
                gm = jnp.broadcast_to(gate[m:m + 1, :], (n_blocks, bs))
                beats = jnp.logical_or(gm > gate, jnp.logical_and(gm == gate, blk_id > m))
                rank = rank + jnp.where(beats, 1.0, 0.0)
            sel = jnp.logical_and(blk_id < i, rank < MOBA_TOPK)
            bias = jnp.where(sel, 0.0, neg_inf)
            biases = [jnp.broadcast_to(bias[jb:jb + 1, :], (sub, bs)) for jb in range(i)]
        m8 = jnp.max(piece(i) + causal, axis=0)
        for jb in range(i):
            blk_max = jnp.max(piece(jb), axis=0)
            m8 = jnp.maximum(m8, blk_max + biases[jb] if ranked else blk_max)
        m_b = jnp.broadcast_to(jnp.max(m8, axis=0, keepdims=True), (sub, bs))
        l8 = jnp.zeros((sub, bs), F32)
        for jb in range(i + 1):
            if jb < i:
                p = jnp.exp2(piece(jb) - (m_b - biases[jb] if ranked else m_b)[None])
            else:
                p = jnp.exp2(piece(jb) + causal - m_b[None])
            l8 = l8 + jnp.sum(p, axis=0)
            p_ref[jb * bs:(jb + 1) * bs, :] = p.reshape(bs, bs).astype(BF16)
        denom = jnp.sum(l8, axis=0, keepdims=True)
        o_t = jnp.dot(vt_ref[cols, 0:nk], p_ref[0:nk, :], preferred_element_type=F32)
        o_ref[i * bs:(i + 1) * bs, cols] = (o_t / denom).T.astype(o_ref.dtype)


def _attention(kq, vt, km, batch, seq, d):
    n_blocks = seq // MOBA_BLOCK
    w = ATTN_HEADS_PER_STEP * HEAD_DIM
    n_groups = d // w
    return pl.pallas_call(
        functools.partial(_attn_kernel, n_blocks=n_blocks),
        grid=(batch, n_groups),
        in_specs=[
            pl.BlockSpec((seq, w), lambda b, h: (b, n_groups + h)),
            pl.BlockSpec((seq, w), lambda b, h: (b, h)),
            pl.BlockSpec((w, seq), lambda b, h: (h, b)),
            pl.BlockSpec((n_blocks, w), lambda b, h: (b, h)),
        ],
        out_specs=pl.BlockSpec((seq, w), lambda b, h: (b, h)),
        out_shape=jax.ShapeDtypeStruct((batch * seq, d), BF16),
        scratch_shapes=[pltpu.VMEM((ATTN_LOOKAHEAD + 1, seq, MOBA_BLOCK), F32),
                        pltpu.VMEM((2, seq, MOBA_BLOCK), BF16)],
        compiler_params=_params("parallel", "arbitrary"),
        name="moba_attn",
    )(kq, kq, vt, km)


def kernel(x, a_norm, a_w_in, a_ln_g, a_ln_b, a_w_s, a_b_s, a_w_out, kv_norm, w_k, w_v, k_norm,
           b_norm, b_w_q, b_q_norm, b_w_o, ffn_norm, ffn_w_gate, ffn_w_up, ffn_w_down):
    batch, seq, d = x.shape
    assert seq % MOBA_BLOCK == 0 and seq // MOBA_BLOCK - 1 >= MOBA_TOPK
    n_a = a_norm.shape[0]
    depth = ffn_norm.shape[0]

    h = x.reshape(batch * seq, d)
    for layer in range(depth):
        if layer < n_a:
            i = layer
            ug = _sgu(h, a_norm[i], a_w_in, i, a_ln_g[i], a_ln_b[i], a_w_s[i], a_b_s[i])
            h = _mm_res(ug, a_w_out, i, h)
        else:
            i = layer - n_a
            assert depth - n_a == 1
            w_kvq = jnp.stack([w_k, w_v, b_w_q[i]]).astype(BF16)
            kq, vt, km = _kvq(h, kv_norm, b_norm[i], w_kvq, k_norm, b_q_norm[i])
            attn = _attention(kq, vt, km, batch, seq, d)
            h = _mm_res(attn, b_w_o, i, h)
        h = _ffn(h, ffn_norm[layer], ffn_w_gate, ffn_w_up, ffn_w_down, layer)
    return h.reshape(batch, seq, d)
```
